```python
import jax
import jax.numpy as jnp
from jax import lax
import numpy as np

D_MODEL = 1024
BATCH = 1
SEQ = 16384
DEPTH = 2

GRID_W = 64
CTX_LEN = 256
N_MIXERS = 4
GROUP_WIDTH = D_MODEL // N_MIXERS
D_MIX = N_MIXERS * GROUP_WIDTH
HEAD_DIM = 64
BLOCK = 128
ROPE_THETA = 10000.0
EPS = 1e-6
NEG_INF = -1e30
N_MOD = 6
A_HEADS = GROUP_WIDTH // HEAD_DIM
A_KV_HEADS = 2
WINDOW = 128
POOL_SIZES = (2, 4, 8, 16)
POOL_GROUP = GROUP_WIDTH // len(POOL_SIZES)
C_NOPE = 64
C_ROPE = 32
C_V = 64
C_HEADS = GROUP_WIDTH // C_V
C_Q_RANK = D_MODEL // 4
C_KV_RANK = D_MODEL // 8
D_HEADS = GROUP_WIDTH // HEAD_DIM
D_KV_HEADS = 2
D_FF = -(-(8 * D_MODEL) // (3 * 256)) * 256

IN_WIDTHS = (A_HEADS * HEAD_DIM, A_KV_HEADS * HEAD_DIM, A_KV_HEADS * HEAD_DIM,
             GROUP_WIDTH,
             C_Q_RANK, C_KV_RANK, C_ROPE,
             D_HEADS * HEAD_DIM, D_KV_HEADS * HEAD_DIM, D_KV_HEADS * HEAD_DIM)
IN_SPLITS = tuple(int(v) for v in np.cumsum(IN_WIDTHS)[:-1])
IN_WIDTH = int(sum(IN_WIDTHS))

kernel_name = 'hybrid_parallel_group_flow_block'


def rms_norm(x, g):
    xf = x.astype(jnp.float32)
    y = xf * lax.rsqrt(jnp.mean(xf * xf, axis=-1, keepdims=True) + EPS)
    return (y * g.astype(jnp.float32)).astype(x.dtype)


def axial_rope(row, col, rot_dim):
    n_axis = rot_dim // 4
    freqs = ROPE_THETA ** (-jnp.arange(n_axis, dtype=jnp.float32) / n_axis)
    ang = jnp.concatenate([row[:, None] * freqs, col[:, None] * freqs], axis=-1)
    return jnp.cos(ang), jnp.sin(ang)


def apply_rope(x, rope):
    cos, sin = rope
    cos = cos[None, :, None, :].astype(x.dtype)
    sin = sin[None, :, None, :].astype(x.dtype)
    half = x.shape[-1] // 2
    x1, x2 = x[..., :half], x[..., half:]
    return jnp.concatenate([x1 * cos - x2 * sin, x2 * cos + x1 * sin], axis=-1)


def softmax_with_sink(s, sink):
    m = jnp.maximum(jnp.max(s, axis=-1, keepdims=True), sink)
    p = jnp.exp(s - m)
    return p / (jnp.sum(p, axis=-1, keepdims=True) + jnp.exp(sink - m))


def modulation(cond, w, b):
    m = jax.nn.silu(cond) @ w + b
    return jnp.split(m[:, None, :], N_MOD, axis=-1)


def modulate(h, shift, scale):
    return h * (1 + scale) + shift


def full_attention(q, k, v, scale, sink=None):
    b, tq, h, d = q.shape
    hk = k.shape[2]
    qg = q.reshape(b, tq, hk, h // hk, d)
    s = jnp.einsum('bqhgd,bkhd->bhgqk', qg, k, preferred_element_type=jnp.float32) * scale
    if sink is None:
        w = jax.nn.softmax(s, axis=-1)
    else:
        w = softmax_with_sink(s, sink.astype(jnp.float32).reshape(1, hk, h // hk, 1, 1))
    o = jnp.einsum('bhgqk,bkhd->bqhgd', w.astype(v.dtype), v)
    return o.reshape(b, tq, h * v.shape[-1])


def dense_attention(q, k, v, k_ctx, v_ctx, scale):
    b, n, h, d = q.shape
    hk = k.shape[2]
    g = h // hk
    keys = jnp.concatenate([k_ctx, k], axis=1)
    vals = jnp.concatenate([v_ctx, v], axis=1)
    qb = jnp.moveaxis(q.reshape(b, n // BLOCK, BLOCK, hk, g, d), 1, 0)

    def one_block(qi):
        s = jnp.einsum('bqhgd,bkhd->bhgqk', qi, keys, preferred_element_type=jnp.float32) * scale
        w = jax.nn.softmax(s, axis=-1)
        return jnp.einsum('bhgqk,bkhd->bqhgd', w.astype(vals.dtype), vals)

    o = lax.map(one_block, qb)
    return jnp.moveaxis(o, 0, 1).reshape(b, n, h * vals.shape[-1])


def window_attention(q, k, v, k_ctx, v_ctx, sink, scale):
    b, n, h, d = q.shape
    hk = k.shape[2]
    g = h // hk
    nb = n // BLOCK

    def band(t):
        tb = t.reshape(b, nb, BLOCK, hk, t.shape[-1])
        tp = jnp.pad(tb, ((0, 0), (1, 1), (0, 0), (0, 0), (0, 0)))
        return jnp.concatenate([tp[:, :-2], tp[:, 1:-1], tp[:, 2:]], axis=2)

    kb, vb = band(k), band(v)
    qb = q.reshape(b, nb, BLOCK, hk, g, d)
    blk = jnp.arange(nb)[:, None]
    q_pos = blk * BLOCK + jnp.arange(BLOCK)[None, :]
    k_pos = (blk - 1) * BLOCK + jnp.arange(3 * BLOCK)[None, :]
    valid = ((jnp.abs(q_pos[:, :, None] - k_pos[:, None, :]) <= WINDOW)
             & (k_pos[:, None, :] >= 0) & (k_pos[:, None, :] < n))
    s_loc = jnp.einsum('bnqhgd,bnkhd->bnhgqk', qb, kb, preferred_element_type=jnp.float32) * scale
    s_loc = jnp.where(valid[None, :, None, None], s_loc, NEG_INF)
    s_ctx = jnp.einsum('bnqhgd,bkhd->bnhgqk', qb, k_ctx, preferred_element_type=jnp.float32) * scale
    n_ctx = k_ctx.shape[1]
    w = softmax_with_sink(jnp.concatenate([s_ctx, s_loc], axis=-1),
                          sink.astype(jnp.float32).reshape(1, 1, hk, g, 1, 1)).astype(v.dtype)
    o = (jnp.einsum('bnhgqk,bkhd->bnqhgd', w[..., :n_ctx], v_ctx)
         + jnp.einsum('bnhgqk,bnkhd->bnqhgd', w[..., n_ctx:], vb))
    return o.reshape(b, n, h * v.shape[-1])


def multiscale_pool(u, w_pool, scale):
    b, t, _ = u.shape
    uf = u.astype(jnp.float32).reshape(b, t, len(POOL_SIZES), POOL_GROUP)
    csum = jnp.pad(jnp.cumsum(uf, axis=1), ((0, 0), (1, 0), (0, 0), (0, 0)))
    pos = jnp.arange(t)
    outs = []
    for gi, size in enumerate(POOL_SIZES):
        lo = jnp.clip(pos - size // 2, 0, t)
        hi = jnp.clip(pos - size // 2 + size, 0, t)
        mean = (csum[:, hi, gi] - csum[:, lo, gi]) / (hi - lo).astype(jnp.float32)[None, :, None]
        outs.append(mean - uf[:, :, gi])
    y = jnp.stack(outs, axis=2).astype(u.dtype)
    y = jnp.einsum('btgc,gce->btge', y, w_pool).reshape(b, t, GROUP_WIDTH)
    return y * scale


def mixer_inputs(h, w_in, c_gq, c_wuq, c_gkv, c_wukv, d_gq, d_gk, rope_hd, rope_c):
    b, t, _ = h.shape
    z = h @ w_in
    qa, ka, va, ub, cq, ckv, kr, qd, kd, vd = jnp.split(z, IN_SPLITS, axis=-1)
    qa = qa.reshape(b, t, A_HEADS, HEAD_DIM)
    ka = ka.reshape(b, t, A_KV_HEADS, HEAD_DIM)
    va = va.reshape(b, t, A_KV_HEADS, HEAD_DIM)
    qc = (rms_norm(cq, c_gq) @ c_wuq).reshape(b, t, C_HEADS, C_NOPE + C_ROPE)
    q_nope, q_rope = qc[..., :C_NOPE], qc[..., C_NOPE:]
    kvc = (rms_norm(ckv, c_gkv) @ c_wukv).reshape(b, t, C_HEADS, C_NOPE + C_V)
    k_nope, vc = kvc[..., :C_NOPE], kvc[..., C_NOPE:]
    k_rope = kr.reshape(b, t, 1, C_ROPE)
    qd = rms_norm(qd.reshape(b, t, D_HEADS, HEAD_DIM), d_gq)
    kd = rms_norm(kd.reshape(b, t, D_KV_HEADS, HEAD_DIM), d_gk)
    vd = vd.reshape(b, t, D_KV_HEADS, HEAD_DIM)
    if rope_hd is not None:
        qa, ka = apply_rope(qa, rope_hd), apply_rope(ka, rope_hd)
        qd, kd = apply_rope(qd, rope_hd), apply_rope(kd, rope_hd)
        q_rope, k_rope = apply_rope(q_rope, rope_c), apply_rope(k_rope, rope_c)
    qc = jnp.concatenate([q_nope, q_rope], axis=-1)
    kc = jnp.concatenate([k_nope, jnp.broadcast_to(k_rope, (b, t, C_HEADS, C_ROPE))], axis=-1)
    return qa, ka, va, ub, qc, kc, vc, qd, kd, vd


def swiglu(h, w_gu, w_down):
    gate, up = jnp.split(h @ w_gu, 2, axis=-1)
    return (jax.nn.silu(gate) * up) @ w_down


def setup_inputs(seed: int = 0) -> dict:
    key = jax.random.key(seed)
    ks = jax.random.split(key, 23)
    f32 = jnp.float32
    L, D = DEPTH, D_MODEL

    def nrm(k, shape, scale):
        return jax.random.normal(k, shape, f32) * scale

    def gain(k, shape):
        return 1.0 + 0.02 * jax.random.normal(k, shape, f32)

    return {
        'x': nrm(ks[0], (BATCH, SEQ, D), 1.0),
        'c': nrm(ks[1], (BATCH, D), 1.0),
        'ctx': nrm(ks[2], (BATCH, CTX_LEN, D), 1.0),
        'c_ctx': nrm(ks[3], (D,), 1.0),
        'w_ada': nrm(ks[4], (L, D, N_MOD * D), 0.5 * D ** -0.5),
        'b_ada': nrm(ks[5], (L, N_MOD * D), 0.02),
        'g_pre1': gain(ks[6], (L, D)),
        'g_post1': gain(ks[7], (L, D)),
        'w_in': nrm(ks[8], (L, D, IN_WIDTH), D ** -0.5),
        'a_sink': nrm(ks[9], (L, A_HEADS), 1.0),
        'pool_w': nrm(ks[10], (L, len(POOL_SIZES), POOL_GROUP, POOL_GROUP), POOL_GROUP ** -0.5),
        'pool_scale': gain(ks[11], (L, GROUP_WIDTH)),
        'c_gq': gain(ks[12], (L, C_Q_RANK)),
        'c_wuq': nrm(ks[13], (L, C_Q_RANK, C_HEADS * (C_NOPE + C_ROPE)), C_Q_RANK ** -0.5),
        'c_gkv': gain(ks[14], (L, C_KV_RANK)),
        'c_wukv': nrm(ks[15], (L, C_KV_RANK, C_HEADS * (C_NOPE + C_V)), C_KV_RANK ** -0.5),
        'd_gq': gain(ks[16], (L, HEAD_DIM)),
        'd_gk': gain(ks[17], (L, HEAD_DIM)),
        'w_out': nrm(ks[18], (L, D_MIX, D), D_MIX ** -0.5),
        'g_pre2': gain(ks[19], (L, D)),
        'g_post2': gain(ks[20], (L, D)),
        'w_gu': nrm(ks[21], (L, D, 2 * D_FF), D ** -0.5),
        'w_down': nrm(ks[22], (L, D_FF, D), D_FF ** -0.5),
    }


def reference(x, c, ctx, c_ctx, w_ada, b_ada, g_pre1, g_post1, w_in, a_sink, pool_w, pool_scale,
              c_gq, c_wuq, c_gkv, c_wukv, d_gq, d_gk, w_out, g_pre2, g_post2, w_gu, w_down):
    n = x.shape[1]
    rows = n // GRID_W
    row = jnp.repeat(jnp.arange(rows, dtype=jnp.float32), GRID_W)
    col = jnp.tile(jnp.arange(GRID_W, dtype=jnp.float32), rows)
    rope_hd = axial_rope(row, col, HEAD_DIM)
    rope_c = axial_rope(row, col, C_ROPE)
    scale_hd = HEAD_DIM ** -0.5
    scale_c = (C_NOPE + C_ROPE) ** -0.5

    for l in range(DEPTH):
        last = l == DEPTH - 1
        mx = modulation(c, w_ada[l], b_ada[l])
        mc = modulation(c_ctx[None], w_ada[l], b_ada[l])
        proj = (w_in[l], c_gq[l], c_wuq[l], c_gkv[l], c_wukv[l], d_gq[l], d_gk[l])

        hx = modulate(rms_norm(x, g_pre1[l]), mx[0], mx[1])
        hc = modulate(rms_norm(ctx, g_pre1[l]), mc[0], mc[1])
        qa, ka, va, ub, qc, kc, vc, qd, kd, vd = mixer_inputs(hx, *proj, rope_hd, rope_c)
        qa_c, ka_c, va_c, ub_c, qc_c, kc_c, vc_c, qd_c, kd_c, vd_c = mixer_inputs(hc, *proj, None, None)

        ya = window_attention(qa, ka, va, ka_c, va_c, a_sink[l], scale_hd)
        yb = multiscale_pool(ub, pool_w[l], pool_scale[l])
        yc = dense_attention(qc, kc, vc, kc_c, vc_c, scale_c)
        yd = dense_attention(qd, kd, vd, kd_c, vd_c, scale_hd)
        y = jnp.concatenate([ya, yb, yc, yd], axis=-1) @ w_out[l]
        x = x + mx[2] * rms_norm(y, g_post1[l])

        if not last:
            ya_c = full_attention(qa_c, ka_c, va_c, scale_hd, a_sink[l])
            yb_c = multiscale_pool(ub_c, pool_w[l], pool_scale[l])
            yc_c = full_attention(qc_c, kc_c, vc_c, scale_c)
            yd_c = full_attention(qd_c, kd_c, vd_c, scale_hd)
            y_c = jnp.concatenate([ya_c, yb_c, yc_c, yd_c], axis=-1) @ w_out[l]
            ctx = ctx + mc[2] * rms_norm(y_c, g_post1[l])

        hx = modulate(rms_norm(x, g_pre2[l]), mx[3], mx[4])
        x = x + mx[5] * rms_norm(swiglu(hx, w_gu[l], w_down[l]), g_post2[l])
        if not last:
            hc = modulate(rms_norm(ctx, g_pre2[l]), mc[3], mc[4])
            ctx = ctx + mc[5] * rms_norm(swiglu(hc, w_gu[l], w_down[l]), g_post2[l])

    return x
```

```python
import functools

import numpy as np
import jax
import jax.numpy as jnp
from jax import lax
from jax.experimental import pallas as pl
from jax.experimental.pallas import tpu as pltpu

F32 = jnp.float32
BF16 = jnp.bfloat16

D_MODEL = 1024
GRID_W = 64
GROUP_WIDTH = 256
HEAD_DIM = 64
ROPE_THETA = 10000.0
EPS = 1e-6
NEG_INF = -1e30
N_MOD = 6
N_HEADS = 4
KV_HEADS = 2
WINDOW = 128
POOL_SIZES = (2, 4, 8, 16)
POOL_GROUP = 64
C_NOPE, C_ROPE, C_V = 64, 32, 64
C_Q_RANK, C_KV_RANK = 256, 128
D_FF = 2816
IN_WIDTH = 1696
KR_END = 1184
IN_PACKED = 1792

LANE = 128
QK_PAD = 128
V_ROWS = 80
ROW_TILE = 256
Q_TILE = 512
KV_TILE = 1024
HALO = 128
VMEM_LIMIT = 56 * 1024 * 1024


def _cparams(sem):
    return pltpu.CompilerParams(dimension_semantics=sem, vmem_limit_bytes=VMEM_LIMIT)


def _rms(x, g):
    ms = jnp.mean(x * x, axis=-1, keepdims=True)
    return x * lax.rsqrt(ms + EPS) * g


def _split_bf16(x):
    hi = x.astype(BF16)
    lo = (x - hi.astype(F32)).astype(BF16)
    return hi, lo


def _mod_kernel(cond_ref, w_ref, b_ref, o_ref):
    c = cond_ref[...]
    s = c * jax.nn.sigmoid(c)
    s_hi, s_lo = _split_bf16(s)
    w_hi, w_lo = _split_bf16(w_ref[0])
    acc = jnp.dot(s_hi, w_hi, preferred_element_type=F32)
    acc += jnp.dot(s_hi, w_lo, preferred_element_type=F32)
    acc += jnp.dot(s_lo, w_hi, preferred_element_type=F32)
    o_ref[0] = acc + b_ref[0]


def _modulation(cond, w_ada, b_ada):
    depth, d, n = w_ada.shape
    tn = 1536
    return pl.pallas_call(
        _mod_kernel,
        grid=(depth, n // tn),
        in_specs=[
            pl.BlockSpec((8, d), lambda l, j: (0, 0)),
            pl.BlockSpec((1, d, tn), lambda l, j: (l, 0, j)),
            pl.BlockSpec((1, 1, tn), lambda l, j: (l, 0, j)),
        ],
        out_specs=pl.BlockSpec((1, 8, tn), lambda l, j: (l, 0, j)),
        out_shape=jax.ShapeDtypeStruct((depth, 8, n), F32),
        compiler_params=_cparams(("arbitrary", "arbitrary")),
        name="modulation",
    )(cond, w_ada, b_ada.reshape(depth, 1, n))


def _rope_t(xt, cos, sin):
    half = xt.shape[0] // 2
    x1, x2 = xt[:half], xt[half:]
    return jnp.concatenate([x1 * cos - x2 * sin, x2 * cos + x1 * sin], axis=0)


def _ones_rows(r):
    row = lax.broadcasted_iota(jnp.int32, (V_ROWS - HEAD_DIM, r), 0)
    return jnp.where(row == 0, 1.0, 0.0).astype(BF16)


def _proj_kernel(x_ref, mod_ref, gpre_ref, win_ref, cgq_ref, cwuq_ref, cgkv_ref, cwukv_ref,
                 dgq_ref, dgk_ref, cos_hd_ref, sin_hd_ref, cos_c_ref, sin_c_ref,
                 qta_ref, ka_ref, vta_ref, u_ref, qtc_ref, kc_ref, vtc_ref, qtd_ref, kd_ref, vtd_ref):
    r = x_ref.shape[0]
    scale_hd = HEAD_DIM ** -0.5
    scale_c = (C_NOPE + C_ROPE) ** -0.5
    h = _rms(x_ref[...], gpre_ref[0])
    h = h * (1.0 + mod_ref[0, 0, 1:2, :]) + mod_ref[0, 0, 0:1, :]
    z = jnp.dot(h.astype(BF16), win_ref[0], preferred_element_type=F32)

    cos_hd, sin_hd = cos_hd_ref[...], sin_hd_ref[...]
    cos_c, sin_c = cos_c_ref[...], sin_c_ref[...]
    ones = _ones_rows(r)
    zeros64 = jnp.zeros((HEAD_DIM, r), BF16)

    def put_gqa_q(ref, qt, hd):
        g = hd // (N_HEADS // KV_HEADS)
        ref[hd, g * HEAD_DIM:(g + 1) * HEAD_DIM, :] = (qt * scale_hd).astype(BF16)
        ref[hd, (1 - g) * HEAD_DIM:(2 - g) * HEAD_DIM, :] = zeros64

    def put_v(ref, hd, vt):
        ref[hd, 0:HEAD_DIM, :] = vt.astype(BF16)
        ref[hd, HEAD_DIM:V_ROWS, :] = ones

    qa_t = z[:, 0:256].T
    for hd in range(N_HEADS):
        put_gqa_q(qta_ref, _rope_t(qa_t[hd * 64:(hd + 1) * 64], cos_hd, sin_hd), hd)
    ka_t = z[:, 256:384].T
    ka_t = jnp.concatenate([_rope_t(ka_t[g * 64:(g + 1) * 64], cos_hd, sin_hd) for g in range(KV_HEADS)], axis=0)
    ka_ref[0] = ka_t.T.astype(BF16)
    va_t = z[:, 384:512].T
    for g in range(KV_HEADS):
        put_v(vta_ref, g, va_t[g * 64:(g + 1) * 64])

    u_ref[...] = z[:, 512:768]

    cq = _rms(z[:, 768:1024], cgq_ref[0])
    qc_t = jnp.dot(cq.astype(BF16), cwuq_ref[0], preferred_element_type=F32).T
    ckv = _rms(z[:, 1024:1152], cgkv_ref[0])
    kvc_t = jnp.dot(ckv.astype(BF16), cwukv_ref[0], preferred_element_type=F32).T
    kr_t = _rope_t(z[:, 1152:1280].T[0:C_ROPE], cos_c, sin_c)
    pad32 = jnp.zeros((QK_PAD - C_NOPE - C_ROPE, r), F32)
    per_head_q = C_NOPE + C_ROPE
    for hd in range(N_HEADS):
        q_nope = qc_t[hd * per_head_q:hd * per_head_q + C_NOPE]
        q_rope = _rope_t(qc_t[hd * per_head_q + C_NOPE:(hd + 1) * per_head_q], cos_c, sin_c)
        qtc_ref[hd] = (jnp.concatenate([q_nope, q_rope, pad32], axis=0) * scale_c).astype(BF16)
        k_nope = kvc_t[hd * 128:hd * 128 + C_NOPE]
        kc_ref[hd] = jnp.concatenate([k_nope, kr_t, pad32], axis=0).T.astype(BF16)
        put_v(vtc_ref, hd, kvc_t[hd * 128 + C_NOPE:(hd + 1) * 128])

    def head_norm(xt, g_col):
        ms = jnp.mean(xt * xt, axis=0, keepdims=True)
        return xt * lax.rsqrt(ms + EPS) * g_col

    qd_t = z[:, 1280:1536].T
    for hd in range(N_HEADS):
        q = head_norm(qd_t[hd * 64:(hd + 1) * 64], dgq_ref[0])
        put_gqa_q(qtd_ref, _rope_t(q, cos_hd, sin_hd), hd)
    kd_t = z[:, 1536:1664].T
    kd_t = jnp.concatenate(
        [_rope_t(head_norm(kd_t[g * 64:(g + 1) * 64], dgk_ref[0]), cos_hd, sin_hd) for g in range(KV_HEADS)], axis=0)
    kd_ref[0] = kd_t.T.astype(BF16)
    vd_t = z[:, 1664:1792].T
    for g in range(KV_HEADS):
        put_v(vtd_ref, g, vd_t[g * 64:(g + 1) * 64])


def _project(x, mod, stream, layer, p, rope):
    t = x.shape[0]
    r = ROW_TILE
    cos_hd, sin_hd, cos_c, sin_c = rope
    lay3 = lambda i: (layer, 0, 0)
    tok = lambda i: (i, 0)
    feat3 = lambda i: (0, 0, i)
    tok3 = lambda i: (0, i, 0)
    out_shapes = (
        jax.ShapeDtypeStruct((N_HEADS, QK_PAD, t), BF16),
        jax.ShapeDtypeStruct((1, t, QK_PAD), BF16),
        jax.ShapeDtypeStruct((KV_HEADS, V_ROWS, t), BF16),
        jax.ShapeDtypeStruct((t, GROUP_WIDTH), F32),
        jax.ShapeDtypeStruct((N_HEADS, QK_PAD, t), BF16),
        jax.ShapeDtypeStruct((N_HEADS, t, QK_PAD), BF16),
        jax.ShapeDtypeStruct((N_HEADS, V_ROWS, t), BF16),
        jax.ShapeDtypeStruct((N_HEADS, QK_PAD, t), BF16),
        jax.ShapeDtypeStruct((1, t, QK_PAD), BF16),
        jax.ShapeDtypeStruct((KV_HEADS, V_ROWS, t), BF16),
    )
    out_specs = (
        pl.BlockSpec((N_HEADS, QK_PAD, r), feat3),
        pl.BlockSpec((1, r, QK_PAD), tok3),
        pl.BlockSpec((KV_HEADS, V_ROWS, r), feat3),
        pl.BlockSpec((r, GROUP_WIDTH), tok),
        pl.BlockSpec((N_HEADS, QK_PAD, r), feat3),
        pl.BlockSpec((N_HEADS, r, QK_PAD), tok3),
        pl.BlockSpec((N_HEADS, V_ROWS, r), feat3),
        pl.BlockSpec((N_HEADS, QK_PAD, r), feat3),
        pl.BlockSpec((1, r, QK_PAD), tok3),
        pl.BlockSpec((KV_HEADS, V_ROWS, r), feat3),
    )
    in_specs = [
        pl.BlockSpec((r, D_MODEL), tok),
        pl.BlockSpec((1, 1, N_MOD, D_MODEL), lambda i: (layer, stream, 0, 0)),
        pl.BlockSpec((1, 1, D_MODEL), lay3),
        pl.BlockSpec((1, D_MODEL, IN_PACKED), lay3),
        pl.BlockSpec((1, 1, C_Q_RANK), lay3),
        pl.BlockSpec((1, C_Q_RANK, p["c_wuq"].shape[2]), lay3),
        pl.BlockSpec((1, 1, C_KV_RANK), lay3),
        pl.BlockSpec((1, C_KV_RANK, p["c_wukv"].shape[2]), lay3),
        pl.BlockSpec((1, HEAD_DIM, 1), lay3),
        pl.BlockSpec((1, HEAD_DIM, 1), lay3),
        pl.BlockSpec((HEAD_DIM // 2, r), lambda i: (0, i)),
        pl.BlockSpec((HEAD_DIM // 2, r), lambda i: (0, i)),
        pl.BlockSpec((C_ROPE // 2, r), lambda i: (0, i)),
        pl.BlockSpec((C_ROPE // 2, r), lambda i: (0, i)),
    ]
    return pl.pallas_call(
        _proj_kernel,
        grid=(t // r,),
        in_specs=in_specs,
        out_specs=out_specs,
        out_shape=out_shapes,
        compiler_params=_cparams(("arbitrary",)),
        name="in_proj",
    )(x, mod, p["g_pre1"], p["w_in"], p["c_gq"], p["c_wuq"], p["c_gkv"], p["c_wukv"],
      p["d_gq"], p["d_gk"], cos_hd, sin_hd, cos_c, sin_c)


def _flash_kernel(*refs, n_lat_steps, has_sink):
    refs = list(refs)
    sink_ref = refs.pop(0) if has_sink else None
    qt_ref, kc_ref, vtc_ref = refs[:3]
    refs = refs[3:]
    if n_lat_steps:
        kl_ref, vtl_ref = refs[:2]
        refs = refs[2:]
    o_ref, m_ref, acc_ref = refs

    q = qt_ref[0]
    s = jnp.dot(kc_ref[0], q, preferred_element_type=F32)
    m = jnp.max(s, axis=0, keepdims=True)
    if has_sink:
        sink = sink_ref[pl.program_id(0)]
        m = jnp.maximum(m, sink)
    p = jnp.exp(s - m)
    acc = jnp.dot(vtc_ref[0], p.astype(BF16), preferred_element_type=F32)
    if has_sink:
        row = lax.broadcasted_iota(jnp.int32, acc.shape, 0)
        acc = acc + jnp.where(row == HEAD_DIM, jnp.exp(sink - m), 0.0)

    if n_lat_steps:
        m_ref[...] = m
        acc_ref[...] = acc

        def body(j, carry):
            off = pl.multiple_of(j * KV_TILE, KV_TILE)
            sj = jnp.dot(kl_ref[0, pl.ds(off, KV_TILE), :], q, preferred_element_type=F32)
            m_old = m_ref[...]
            m_new = jnp.maximum(m_old, jnp.max(sj, axis=0, keepdims=True))
            pj = jnp.exp(sj - m_new)
            alpha = jnp.exp(m_old - m_new)
            vj = vtl_ref[0, :, pl.ds(off, KV_TILE)]
            acc_ref[...] = alpha * acc_ref[...] + jnp.dot(vj, pj.astype(BF16), preferred_element_type=F32)
            m_ref[...] = m_new
            return carry

        lax.fori_loop(0, n_lat_steps, body, 0)
        acc = acc_ref[...]
    o_ref[...] = (acc[0:HEAD_DIM] / acc[HEAD_DIM:HEAD_DIM + 1]).astype(o_ref.dtype)


def _dense_attention(qt, k_ctx, vt_ctx, k_lat=None, vt_lat=None, sink=None, name="dense_attn"):
    n_heads, _, tq_total = qt.shape
    tq = min(Q_TILE, tq_total)
    n_ctx = k_ctx.shape[1]
    k_rep = n_heads // k_ctx.shape[0]
    v_rep = n_heads // vt_ctx.shape[0]
    has_sink = sink is not None
    has_lat = k_lat is not None
    n_lat = k_lat.shape[1] if has_lat else 0
    args, in_specs = [], []
    if has_sink:
        args.append(sink)
        in_specs.append(pl.BlockSpec(memory_space=pltpu.SMEM))
    args += [qt, k_ctx, vt_ctx]
    in_specs += [
        pl.BlockSpec((1, QK_PAD, tq), lambda h, i: (h, 0, i)),
        pl.BlockSpec((1, n_ctx, QK_PAD), lambda h, i: (h // k_rep, 0, 0)),
        pl.BlockSpec((1, V_ROWS, n_ctx), lambda h, i: (h // v_rep, 0, 0)),
    ]
    if has_lat:
        args += [k_lat, vt_lat]
        in_specs += [
            pl.BlockSpec((1, n_lat, QK_PAD), lambda h, i: (h // k_rep, 0, 0)),
            pl.BlockSpec((1, V_ROWS, n_lat), lambda h, i: (h // v_rep, 0, 0)),
        ]
    return pl.pallas_call(
        functools.partial(_flash_kernel, n_lat_steps=n_lat // KV_TILE, has_sink=has_sink),
        grid=(n_heads, tq_total // tq),
        in_specs=in_specs,
        out_specs=pl.BlockSpec((HEAD_DIM, tq), lambda h, i: (h, i)),
        out_shape=jax.ShapeDtypeStruct((n_heads * HEAD_DIM, tq_total), BF16),
        scratch_shapes=[pltpu.VMEM((1, tq), F32), pltpu.VMEM((V_ROWS, tq), F32)],
        compiler_params=_cparams(("arbitrary", "arbitrary")),
        name=name,
    )(*args)


def _window_kernel(sink_ref, qt_ref, kc_ref, vtc_ref, kp_ref, km_ref, kn_ref, vp_ref, vm_ref, vn_ref,
                   bias_ref, o_ref):
    i = pl.program_id(0)
    n = pl.num_programs(0)
    sink = sink_ref[pl.program_id(1)]
    q = qt_ref[0]
    dot = functools.partial(jnp.dot, preferred_element_type=F32)
    edge_prev = jnp.where(i == 0, NEG_INF, 0.0).astype(F32)
    edge_next = jnp.where(i == n - 1, NEG_INF, 0.0).astype(F32)
    s_c = dot(kc_ref[0], q)
    s_p = dot(kp_ref[0], q) + (bias_ref[0:HALO, :] + edge_prev)
    s_m = dot(km_ref[0], q) + bias_ref[HALO:HALO + Q_TILE, :]
    s_n = dot(kn_ref[0], q) + (bias_ref[HALO + Q_TILE:, :] + edge_next)
    m = jnp.maximum(jnp.maximum(jnp.max(s_c, axis=0, keepdims=True), jnp.max(s_p, axis=0, keepdims=True)),
                    jnp.maximum(jnp.max(s_m, axis=0, keepdims=True), jnp.max(s_n, axis=0, keepdims=True)))
    m = jnp.maximum(m, sink)
    pv = lambda v_ref, s: dot(v_ref[0], jnp.exp(s - m).astype(BF16))
    acc = pv(vtc_ref, s_c) + pv(vp_ref, s_p) + pv(vm_ref, s_m) + pv(vn_ref, s_n)
    denom = acc[HEAD_DIM:HEAD_DIM + 1] + jnp.exp(sink - m)
    o_ref[...] = (acc[0:HEAD_DIM] / denom).astype(o_ref.dtype)


def _window_bias():
    r = np.arange(Q_TILE + 2 * HALO)[:, None] - HALO
    c = np.arange(Q_TILE)[None, :]
    return np.where(np.abs(c - r) <= WINDOW, 0.0, NEG_INF).astype(np.float32)


def _window_attention(qt, k_ctx, vt_ctx, k_lat, vt_lat, sink):
    n_heads, _, t = qt.shape
    n_ctx = k_ctx.shape[1]
    v_rep = n_heads // vt_ctx.shape[0]
    per = Q_TILE // HALO
    last = t // HALO - 1
    prev_blk = lambda i: jnp.maximum(i * per - 1, 0)
    next_blk = lambda i: jnp.minimum(i * per + per, last)
    in_specs = [
        pl.BlockSpec(memory_space=pltpu.SMEM),
        pl.BlockSpec((1, QK_PAD, Q_TILE), lambda i, h: (h, 0, i)),
        pl.BlockSpec((1, n_ctx, QK_PAD), lambda i, h: (0, 0, 0)),
        pl.BlockSpec((1, V_ROWS, n_ctx), lambda i, h: (h // v_rep, 0, 0)),
        pl.BlockSpec((1, HALO, QK_PAD), lambda i, h: (0, prev_blk(i), 0)),
        pl.BlockSpec((1, Q_TILE, QK_PAD), lambda i, h: (0, i, 0)),
        pl.BlockSpec((1, HALO, QK_PAD), lambda i, h: (0, next_blk(i), 0)),
        pl.BlockSpec((1, V_ROWS, HALO), lambda i, h: (h // v_rep, 0, prev_blk(i))),
        pl.BlockSpec((1, V_ROWS, Q_TILE), lambda i, h: (h // v_rep, 0, i)),
        pl.BlockSpec((1, V_ROWS, HALO), lambda i, h: (h // v_rep, 0, next_blk(i))),
        pl.BlockSpec((Q_TILE + 2 * HALO, Q_TILE), lambda i, h: (0, 0)),
    ]
    return pl.pallas_call(
        _window_kernel,
        grid=(t // Q_TILE, n_heads),
        in_specs=in_specs,
        out_specs=pl.BlockSpec((HEAD_DIM, Q_TILE), lambda i, h: (h, i)),
        out_shape=jax.ShapeDtypeStruct((n_heads * HEAD_DIM, t), BF16),
        compiler_params=_cparams(("arbitrary", "arbitrary")),
        name="window_attn",
    )(sink, qt, k_ctx, vt_ctx, k_lat, k_lat, k_lat, vt_lat, vt_lat, vt_lat, jnp.asarray(_window_bias()))


def _pool_kernel(u_ref, up_ref, un_ref, band_ref, w_ref, scale_ref, o_ref, *, t_total):
    i = pl.program_id(0)
    n = pl.num_programs(0)
    r = u_ref.shape[0]
    u = u_ref[...]
    prev = jnp.where(i > 0, up_ref[...], 0.0)
    nxt = jnp.where(i < n - 1, un_ref[...], 0.0)
    ext_hi, ext_lo = _split_bf16(jnp.concatenate([prev, u, nxt], axis=0))
    lane = lax.broadcasted_iota(jnp.int32, (r, GROUP_WIDTH), 1)
    pos = lax.broadcasted_iota(jnp.int32, (r, GROUP_WIDTH), 0) + i * r
    total = jnp.zeros((r, GROUP_WIDTH), F32)
    half = jnp.zeros((r, GROUP_WIDTH), jnp.int32)
    for gi, size in enumerate(POOL_SIZES):
        band = band_ref[gi]
        w = jnp.dot(band, ext_hi, preferred_element_type=F32) + jnp.dot(band, ext_lo, preferred_element_type=F32)
        in_group = (lane >= gi * POOL_GROUP) & (lane < (gi + 1) * POOL_GROUP)
        total = jnp.where(in_group, w, total)
        half = jnp.where(in_group, size // 2, half)
    count = jnp.minimum(pos + half, t_total) - jnp.maximum(pos - half, 0)
    y = total / count.astype(F32) - u
    o_ref[...] = (jnp.dot(y.astype(BF16), w_ref[0], preferred_element_type=F32) * scale_ref[0]).astype(o_ref.dtype)


def _pool_bands():
    i = np.arange(ROW_TILE)[:, None]
    j = np.arange(ROW_TILE + 2 * HALO)[None, :] - HALO
    return np.stack([((j >= i - s // 2) & (j < i + s // 2)) for s in POOL_SIZES]).astype(np.float32)


def _pool(u, layer, p):
    t = u.shape[0]
    r = ROW_TILE
    per = r // HALO
    last = t // HALO - 1
    return pl.pallas_call(
        functools.partial(_pool_kernel, t_total=t),
        grid=(t // r,),
        in_specs=[
            pl.BlockSpec((r, GROUP_WIDTH), lambda i: (i, 0)),
            pl.BlockSpec((HALO, GROUP_WIDTH), lambda i: (jnp.maximum(i * per - 1, 0), 0)),
            pl.BlockSpec((HALO, GROUP_WIDTH), lambda i: (jnp.minimum(i * per + per, last), 0)),
            pl.BlockSpec((len(POOL_SIZES), r, r + 2 * HALO), lambda i: (0, 0, 0)),
            pl.BlockSpec((1, GROUP_WIDTH, GROUP_WIDTH), lambda i: (layer, 0, 0)),
            pl.BlockSpec((1, 1, GROUP_WIDTH), lambda i: (layer, 0, 0)),
        ],
        out_specs=pl.BlockSpec((r, GROUP_WIDTH), lambda i: (i, 0)),
        out_shape=jax.ShapeDtypeStruct((t, GROUP_WIDTH), BF16),
        compiler_params=_cparams(("arbitrary",)),
        name="pool",
    )(u, u, u, jnp.asarray(_pool_bands(), BF16), p["pool_bd"], p["pool_scale"])


def _outproj_kernel(x_ref, ota_ref, otc_ref, otd_ref, yb_ref, w_ref, mod_ref, gpost_ref, o_ref):
    ot = jnp.concatenate([ota_ref[...], otc_ref[...], otd_ref[...]], axis=0)
    n_att = ot.shape[0]
    y = lax.dot_general(ot, w_ref[0, 0:n_att, :], (((0,), (0,)), ((), ())), preferred_element_type=F32)
    y += jnp.dot(yb_ref[...], w_ref[0, n_att:, :], preferred_element_type=F32)
    o_ref[...] = x_ref[...] + mod_ref[0, 0, 2:3, :] * _rms(y, gpost_ref[0])


def _out_project(x, ot_a, ot_c, ot_d, yb, mod, stream, layer, p):
    t = x.shape[0]
    r = ROW_TILE
    feat = pl.BlockSpec((GROUP_WIDTH, r), lambda i: (0, i))
    return pl.pallas_call(
        _outproj_kernel,
        grid=(t // r,),
        in_specs=[
            pl.BlockSpec((r, D_MODEL), lambda i: (i, 0)),
            feat, feat, feat,
            pl.BlockSpec((r, GROUP_WIDTH), lambda i: (i, 0)),
            pl.BlockSpec((1, D_MODEL, D_MODEL), lambda i: (layer, 0, 0)),
            pl.BlockSpec((1, 1, N_MOD, D_MODEL), lambda i: (layer, stream, 0, 0)),
            pl.BlockSpec((1, 1, D_MODEL), lambda i: (layer, 0, 0)),
        ],
        out_specs=pl.BlockSpec((r, D_MODEL), lambda i: (i, 0)),
        out_shape=jax.ShapeDtypeStruct((t, D_MODEL), F32),
        compiler_params=_cparams(("arbitrary",)),
        name="out_proj",
    )(x, ot_a, ot_c, ot_d, yb, p["w_out"], mod, p["g_post1"])


def _ffn_kernel(x_ref, mod_ref, gpre_ref, wgu_ref, wdown_ref, gpost_ref, o_ref):
    x = x_ref[...]
    h = _rms(x, gpre_ref[0])
    h = h * (1.0 + mod_ref[0, 0, 4:5, :]) + mod_ref[0, 0, 3:4, :]
    gu = jnp.dot(h.astype(BF16), wgu_ref[0], preferred_element_type=F32)
    gate, up = gu[:, :D_FF], gu[:, D_FF:]
    act = (gate * jax.nn.sigmoid(gate) * up).astype(BF16)
    y = jnp.dot(act, wdown_ref[0], preferred_element_type=F32)
    o_ref[...] = x + mod_ref[0, 0, 5:6, :] * _rms(y, gpost_ref[0])


def _ffn(x, mod, stream, layer, p):
    t = x.shape[0]
    r = ROW_TILE
    return pl.pallas_call(
        _ffn_kernel,
        grid=(t // r,),
        in_specs=[
            pl.BlockSpec((r, D_MODEL), lambda i: (i, 0)),
            pl.BlockSpec((1, 1, N_MOD, D_MODEL), lambda i: (layer, stream, 0, 0)),
            pl.BlockSpec((1, 1, D_MODEL), lambda i: (layer, 0, 0)),
            pl.BlockSpec((1, D_MODEL, 2 * D_FF), lambda i: (layer, 0, 0)),
            pl.BlockSpec((1, D_FF, D_MODEL), lambda i: (layer, 0, 0)),
            pl.BlockSpec((1, 1, D_MODEL), lambda i: (layer, 0, 0)),
        ],
        out_specs=pl.BlockSpec((r, D_MODEL), lambda i: (i, 0)),
        out_shape=jax.ShapeDtypeStruct((t, D_MODEL), F32),
        compiler_params=_cparams(("arbitrary",)),
        name="swiglu",
    )(x, mod, p["g_pre2"], p["w_gu"], p["w_down"], p["g_post2"])


def _rope_tables(n):
    rows = n // GRID_W
    row = jnp.repeat(jnp.arange(rows, dtype=F32), GRID_W)
    col = jnp.tile(jnp.arange(GRID_W, dtype=F32), rows)
    out = []
    for rot_dim in (HEAD_DIM, C_ROPE):
        n_axis = rot_dim // 4
        freqs = ROPE_THETA ** (-jnp.arange(n_axis, dtype=F32) / n_axis)
        ang = jnp.concatenate([row[:, None] * freqs, col[:, None] * freqs], axis=-1)
        out += [jnp.cos(ang).T, jnp.sin(ang).T]
    return tuple(out)


def _identity_rope(n):
    return (jnp.ones((HEAD_DIM // 2, n), F32), jnp.zeros((HEAD_DIM // 2, n), F32),
            jnp.ones((C_ROPE // 2, n), F32), jnp.zeros((C_ROPE // 2, n), F32))


def _pack_params(w_in, pool_w, w_out, w_gu, w_down, c_wuq, c_wukv):
    depth = w_in.shape[0]
    pad = jnp.zeros((depth, D_MODEL, IN_PACKED - IN_WIDTH), w_in.dtype)
    w_in_p = jnp.concatenate([w_in[:, :, :KR_END], pad, w_in[:, :, KR_END:]], axis=2).astype(BF16)
    eye = jnp.eye(len(POOL_SIZES), dtype=pool_w.dtype)
    pool_bd = jnp.einsum("lgce,gh->lgche", pool_w, eye).reshape(depth, GROUP_WIDTH, GROUP_WIDTH).astype(BF16)
    w_out_p = jnp.concatenate([w_out[:, 0:256], w_out[:, 512:1024], w_out[:, 256:512]], axis=1).astype(BF16)
    return dict(w_in=w_in_p, pool_bd=pool_bd, w_out=w_out_p, w_gu=w_gu.astype(BF16), w_down=w_down.astype(BF16),
                c_wuq=c_wuq.astype(BF16), c_wukv=c_wukv.astype(BF16))


def kernel(x, c, ctx, c_ctx, w_ada, b_ada, g_pre1, g_post1, w_in, a_sink, pool_w, pool_scale, c_gq, c_wuq, c_gkv,
           c_wukv, d_gq, d_gk, w_out, g_pre2, g_post2, w_gu, w_down):
    assert x.shape[0] == 1 and ctx.shape[0] == 1
    depth = w_in.shape[0]
    n = x.shape[1]
    n_ctx = ctx.shape[1]
    p = _pack_params(w_in, pool_w, w_out, w_gu, w_down, c_wuq, c_wukv)
    row3 = lambda a: a.reshape(depth, 1, a.shape[1])
    p.update(g_pre1=row3(g_pre1), g_post1=row3(g_post1), g_pre2=row3(g_pre2), g_post2=row3(g_post2),
             pool_scale=row3(pool_scale), c_gq=row3(c_gq), c_gkv=row3(c_gkv),
             d_gq=d_gq.reshape(depth, HEAD_DIM, 1), d_gk=d_gk.reshape(depth, HEAD_DIM, 1))

    cond = jnp.zeros((8, D_MODEL), F32).at[0].set(c[0]).at[1].set(c_ctx)
    mod = _modulation(cond, w_ada, b_ada).reshape(depth, 8, N_MOD, D_MODEL)

    rope_x = _rope_tables(n)
    rope_c = _identity_rope(n_ctx)
    xs, cs = x[0], ctx[0]
    for layer in range(depth):
        last = layer == depth - 1
        sink = a_sink[layer]
        qta, ka, vta, ub, qtc, kc, vtc, qtd, kd, vtd = _project(xs, mod, 0, layer, p, rope_x)
        qta_c, ka_c, vta_c, ub_c, qtc_c, kc_c, vtc_c, qtd_c, kd_c, vtd_c = _project(cs, mod, 1, layer, p, rope_c)

        ot_a = _window_attention(qta, ka_c, vta_c, ka, vta, sink)
        yb = _pool(ub, layer, p)
        ot_c = _dense_attention(qtc, kc_c, vtc_c, kc, vtc, name="dense_attn_c")
        ot_d = _dense_attention(qtd, kd_c, vtd_c, kd, vtd, name="dense_attn_d")
        xs = _out_project(xs, ot_a, ot_c, ot_d, yb, mod, 0, layer, p)

        if not last:
            ot_a_c = _dense_attention(qta_c, ka_c, vta_c, sink=sink, name="ctx_attn_a")
            yb_c = _pool(ub_c, layer, p)
            ot_c_c = _dense_attention(qtc_c, kc_c, vtc_c, name="ctx_attn_c")
            ot_d_c = _dense_attention(qtd_c, kd_c, vtd_c, name="ctx_attn_d")
            cs = _out_project(cs, ot_a_c, ot_c_c, ot_d_c, yb_c, mod, 1, layer, p)

        xs = _ffn(xs, mod, 0, layer, p)
        if not last:
            cs = _ffn(cs, mod, 1, layer, p)
    return xs[None]
```

```python
import functools

import numpy as np
import jax
import jax.numpy as jnp
from jax import lax
from jax.experimental import pallas as pl
from jax.experimental.pallas import tpu as pltpu

F32 = jnp.float32
BF16 = jnp.bfloat16

D_MODEL = 1024
GRID_W = 64
GROUP_WIDTH = 256
HEAD_DIM = 64
ROPE_THETA = 10000.0
EPS = 1e-6
NEG_INF = -1e30
LOG2E = 1.4426950408889634
N_MOD = 6
N_HEADS = 4
KV_HEADS = 2
WINDOW = 128
POOL_SIZES = (2, 4, 8, 16)
POOL_GROUP = 64
C_NOPE, C_ROPE, C_V = 64, 32, 64
C_Q_RANK, C_KV_RANK = 256, 128
D_FF = 2816
IN_WIDTH = 1696
KR_END = 1184
IN_PACKED = 1792

LANE = 128
QK_PAD = 128
V_ROWS = 80
ROW_TILE = 256
Q_TILE = 512
KV_TILE = 1024
HALO = 128
VMEM_LIMIT = 56 * 1024 * 1024


def _cparams(sem):
    return pltpu.CompilerParams(dimension_semantics=sem, vmem_limit_bytes=VMEM_LIMIT)


def _rms(x, g):
    ms = jnp.mean(x * x, axis=-1, keepdims=True)
    return x * lax.rsqrt(ms + EPS) * g


def _split_bf16(x):
    hi = x.astype(BF16)
    lo = (x - hi.astype(F32)).astype(BF16)
    return hi, lo


def _mod_kernel(cond_ref, w_ref, b_ref, o_ref):
    c = cond_ref[...]
    s = c * jax.nn.sigmoid(c)
    s_hi, s_lo = _split_bf16(s)
    w_hi, w_lo = _split_bf16(w_ref[0])
    acc = jnp.dot(s_hi, w_hi, preferred_element_type=F32)
    acc += jnp.dot(s_hi, w_lo, preferred_element_type=F32)
    acc += jnp.dot(s_lo, w_hi, preferred_element_type=F32)
    o_ref[0] = acc + b_ref[0]


def _modulation(cond, w_ada, b_ada):
    depth, d, n = w_ada.shape
    tn = 1536
    return pl.pallas_call(
        _mod_kernel,
        grid=(depth, n // tn),
        in_specs=[
            pl.BlockSpec((8, d), lambda l, j: (0, 0)),
            pl.BlockSpec((1, d, tn), lambda l, j: (l, 0, j)),
            pl.BlockSpec((1, 1, tn), lambda l, j: (l, 0, j)),
        ],
        out_specs=pl.BlockSpec((1, 8, tn), lambda l, j: (l, 0, j)),
        out_shape=jax.ShapeDtypeStruct((depth, 8, n), F32),
        compiler_params=_cparams(("arbitrary", "arbitrary")),
        name="modulation",
    )(cond, w_ada, b_ada.reshape(depth, 1, n))


def _rope_t(xt, cos, sin):
    half = xt.shape[0] // 2
    x1, x2 = xt[:half], xt[half:]
    return jnp.concatenate([x1 * cos - x2 * sin, x2 * cos + x1 * sin], axis=0)


def _ones_rows(r):
    row = lax.broadcasted_iota(jnp.int32, (V_ROWS - HEAD_DIM, r), 0)
    return jnp.where(row == 0, 1.0, 0.0).astype(BF16)


def _proj_kernel(x_ref, mod_ref, gpre_ref, win_ref, cgq_ref, cwuq_ref, cgkv_ref, cwukv_ref,
                 dgq_ref, dgk_ref, cos_hd_ref, sin_hd_ref, cos_c_ref, sin_c_ref,
                 qta_ref, ka_ref, vta_ref, u_ref, qtc_ref, kc_ref, vtc_ref, qtd_ref, kd_ref, vtd_ref):
    r = x_ref.shape[0]
    scale_hd = HEAD_DIM ** -0.5 * LOG2E
    scale_c = (C_NOPE + C_ROPE) ** -0.5 * LOG2E
    h = _rms(x_ref[...], gpre_ref[0])
    h = h * (1.0 + mod_ref[0, 0, 1:2, :]) + mod_ref[0, 0, 0:1, :]
    z = jnp.dot(h.astype(BF16), win_ref[0], preferred_element_type=F32)

    cos_hd, sin_hd = cos_hd_ref[...], sin_hd_ref[...]
    cos_c, sin_c = cos_c_ref[...], sin_c_ref[...]
    ones = _ones_rows(r)
    zeros64 = jnp.zeros((HEAD_DIM, r), BF16)

    def put_gqa_q(ref, qt, hd):
        g = hd // (N_HEADS // KV_HEADS)
        ref[hd, g * HEAD_DIM:(g + 1) * HEAD_DIM, :] = (qt * scale_hd).astype(BF16)
        ref[hd, (1 - g) * HEAD_DIM:(2 - g) * HEAD_DIM, :] = zeros64

    def put_v(ref, hd, vt):
        ref[hd, 0:HEAD_DIM, :] = vt.astype(BF16)
        ref[hd, HEAD_DIM:V_ROWS, :] = ones

    qa_t = z[:, 0:256].T
    for hd in range(N_HEADS):
        put_gqa_q(qta_ref, _rope_t(qa_t[hd * 64:(hd + 1) * 64], cos_hd, sin_hd), hd)
    ka_t = z[:, 256:384].T
    ka_t = jnp.concatenate([_rope_t(ka_t[g * 64:(g + 1) * 64], cos_hd, sin_hd) for g in range(KV_HEADS)], axis=0)
    ka_ref[0] = ka_t.T.astype(BF16)
    va_t = z[:, 384:512].T
    for g in range(KV_HEADS):
        put_v(vta_ref, g, va_t[g * 64:(g + 1) * 64])

    u_ref[...] = z[:, 512:768]

    cq = _rms(z[:, 768:1024], cgq_ref[0])
    qc_t = jnp.dot(cq.astype(BF16), cwuq_ref[0], preferred_element_type=F32).T
    ckv = _rms(z[:, 1024:1152], cgkv_ref[0])
    kvc_t = jnp.dot(ckv.astype(BF16), cwukv_ref[0], preferred_element_type=F32).T
    kr_t = _rope_t(z[:, 1152:1280].T[0:C_ROPE], cos_c, sin_c)
    pad32 = jnp.zeros((QK_PAD - C_NOPE - C_ROPE, r), F32)
    per_head_q = C_NOPE + C_ROPE
    for hd in range(N_HEADS):
        q_nope = qc_t[hd * per_head_q:hd * per_head_q + C_NOPE]
        q_rope = _rope_t(qc_t[hd * per_head_q + C_NOPE:(hd + 1) * per_head_q], cos_c, sin_c)
        qtc_ref[hd] = (jnp.concatenate([q_nope, q_rope, pad32], axis=0) * scale_c).astype(BF16)
        k_nope = kvc_t[hd * 128:hd * 128 + C_NOPE]
        kc_ref[hd] = jnp.concatenate([k_nope, kr_t, pad32], axis=0).T.astype(BF16)
        put_v(vtc_ref, hd, kvc_t[hd * 128 + C_NOPE:(hd + 1) * 128])

    def head_norm(xt, g_col):
        ms = jnp.mean(xt * xt, axis=0, keepdims=True)
        return xt * lax.rsqrt(ms + EPS) * g_col

    qd_t = z[:, 1280:1536].T
    for hd in range(N_HEADS):
        q = head_norm(qd_t[hd * 64:(hd + 1) * 64], dgq_ref[0])
        put_gqa_q(qtd_ref, _rope_t(q, cos_hd, sin_hd), hd)
    kd_t = z[:, 1536:1664].T
    kd_t = jnp.concatenate(
        [_rope_t(head_norm(kd_t[g * 64:(g + 1) * 64], dgk_ref[0]), cos_hd, sin_hd) for g in range(KV_HEADS)], axis=0)
    kd_ref[0] = kd_t.T.astype(BF16)
    vd_t = z[:, 1664:1792].T
    for g in range(KV_HEADS):
        put_v(vtd_ref, g, vd_t[g * 64:(g + 1) * 64])


def _project(x, mod, stream, layer, p, rope):
    t = x.shape[0]
    r = ROW_TILE
    cos_hd, sin_hd, cos_c, sin_c = rope
    lay3 = lambda i: (layer, 0, 0)
    tok = lambda i: (i, 0)
    feat3 = lambda i: (0, 0, i)
    tok3 = lambda i: (0, i, 0)
    out_shapes = (
        jax.ShapeDtypeStruct((N_HEADS, QK_PAD, t), BF16),
        jax.ShapeDtypeStruct((1, t, QK_PAD), BF16),
        jax.ShapeDtypeStruct((KV_HEADS, V_ROWS, t), BF16),
        jax.ShapeDtypeStruct((t, GROUP_WIDTH), F32),
        jax.ShapeDtypeStruct((N_HEADS, QK_PAD, t), BF16),
        jax.ShapeDtypeStruct((N_HEADS, t, QK_PAD), BF16),
        jax.ShapeDtypeStruct((N_HEADS, V_ROWS, t), BF16),
        jax.ShapeDtypeStruct((N_HEADS, QK_PAD, t), BF16),
        jax.ShapeDtypeStruct((1, t, QK_PAD), BF16),
        jax.ShapeDtypeStruct((KV_HEADS, V_ROWS, t), BF16),
    )
    out_specs = (
        pl.BlockSpec((N_HEADS, QK_PAD, r), feat3),
        pl.BlockSpec((1, r, QK_PAD), tok3),
        pl.BlockSpec((KV_HEADS, V_ROWS, r), feat3),
        pl.BlockSpec((r, GROUP_WIDTH), tok),
        pl.BlockSpec((N_HEADS, QK_PAD, r), feat3),
        pl.BlockSpec((N_HEADS, r, QK_PAD), tok3),
        pl.BlockSpec((N_HEADS, V_ROWS, r), feat3),
        pl.BlockSpec((N_HEADS, QK_PAD, r), feat3),
        pl.BlockSpec((1, r, QK_PAD), tok3),
        pl.BlockSpec((KV_HEADS, V_ROWS, r), feat3),
    )
    in_specs = [
        pl.BlockSpec((r, D_MODEL), tok),
        pl.BlockSpec((1, 1, N_MOD, D_MODEL), lambda i: (layer, stream, 0, 0)),
        pl.BlockSpec((1, 1, D_MODEL), lay3),
        pl.BlockSpec((1, D_MODEL, IN_PACKED), lay3),
        pl.BlockSpec((1, 1, C_Q_RANK), lay3),
        pl.BlockSpec((1, C_Q_RANK, p["c_wuq"].shape[2]), lay3),
        pl.BlockSpec((1, 1, C_KV_RANK), lay3),
        pl.BlockSpec((1, C_KV_RANK, p["c_wukv"].shape[2]), lay3),
        pl.BlockSpec((1, HEAD_DIM, 1), lay3),
        pl.BlockSpec((1, HEAD_DIM, 1), lay3),
        pl.BlockSpec((HEAD_DIM // 2, r), lambda i: (0, i)),
        pl.BlockSpec((HEAD_DIM // 2, r), lambda i: (0, i)),
        pl.BlockSpec((C_ROPE // 2, r), lambda i: (0, i)),
        pl.BlockSpec((C_ROPE // 2, r), lambda i: (0, i)),
    ]
    return pl.pallas_call(
        _proj_kernel,
        grid=(t // r,),
        in_specs=in_specs,
        out_specs=out_specs,
        out_shape=out_shapes,
        compiler_params=_cparams(("arbitrary",)),
        name="in_proj",
    )(x, mod, p["g_pre1"], p["w_in"], p["c_gq"], p["c_wuq"], p["c_gkv"], p["c_wukv"],
      p["d_gq"], p["d_gk"], cos_hd, sin_hd, cos_c, sin_c)


def _flash_kernel(*refs, n_lat_steps, has_sink):
    refs = list(refs)
    sink_ref = refs.pop(0) if has_sink else None
    qt_ref, kc_ref, vtc_ref = refs[:3]
    refs = refs[3:]
    if n_lat_steps:
        kl_ref, vtl_ref = refs[:2]
        refs = refs[2:]
    o_ref, m_ref, acc_ref, sa_ref, sb_ref = refs

    q = qt_ref[0]
    s = jnp.dot(kc_ref[0], q, preferred_element_type=F32)
    m = jnp.max(s, axis=0, keepdims=True)
    if has_sink:
        sink = sink_ref[pl.program_id(0)]
        m = jnp.maximum(m, sink)
    p = jnp.exp2(s - m)
    acc = jnp.dot(vtc_ref[0], p.astype(BF16), preferred_element_type=F32)
    if has_sink:
        row = lax.broadcasted_iota(jnp.int32, acc.shape, 0)
        acc = acc + jnp.where(row == HEAD_DIM, jnp.exp2(sink - m), 0.0)

    if n_lat_steps:
        m_ref[...] = m
        acc_ref[...] = acc

        def scores(j):
            off = pl.multiple_of(j * KV_TILE, KV_TILE)
            return jnp.dot(kl_ref[0, pl.ds(off, KV_TILE), :], q, preferred_element_type=F32)

        def absorb(s_ref, j):
            sj = s_ref[...]
            m_old = m_ref[...]
            m_new = jnp.maximum(m_old, jnp.max(sj, axis=0, keepdims=True))
            pj = jnp.exp2(sj - m_new).astype(BF16)
            off = pl.multiple_of(j * KV_TILE, KV_TILE)
            vj = vtl_ref[0, :, pl.ds(off, KV_TILE)]
            acc_ref[...] = jnp.exp2(m_old - m_new) * acc_ref[...] + jnp.dot(vj, pj, preferred_element_type=F32)
            m_ref[...] = m_new

        sa_ref[...] = scores(0)

        def pair(jj, carry):
            j = 2 * jj
            sb_ref[...] = scores(j + 1)
            absorb(sa_ref, j)
            sa_ref[...] = scores(j + 2)
            absorb(sb_ref, j + 1)
            return carry

        lax.fori_loop(0, n_lat_steps // 2 - 1, pair, 0)
        sb_ref[...] = scores(n_lat_steps - 1)
        absorb(sa_ref, n_lat_steps - 2)
        absorb(sb_ref, n_lat_steps - 1)
        acc = acc_ref[...]
    o_ref[...] = (acc[0:HEAD_DIM] / acc[HEAD_DIM:HEAD_DIM + 1]).astype(o_ref.dtype)


def _dense_attention(qt, k_ctx, vt_ctx, k_lat=None, vt_lat=None, sink=None, name="dense_attn"):
    n_heads, _, tq_total = qt.shape
    tq = min(Q_TILE, tq_total)
    n_ctx = k_ctx.shape[1]
    k_rep = n_heads // k_ctx.shape[0]
    v_rep = n_heads // vt_ctx.shape[0]
    has_sink = sink is not None
    has_lat = k_lat is not None
    n_lat = k_lat.shape[1] if has_lat else 0
    assert n_lat % (2 * KV_TILE) == 0
    args, in_specs = [], []
    if has_sink:
        args.append(sink)
        in_specs.append(pl.BlockSpec(memory_space=pltpu.SMEM))
    args += [qt, k_ctx, vt_ctx]
    in_specs += [
        pl.BlockSpec((1, QK_PAD, tq), lambda h, i: (h, 0, i)),
        pl.BlockSpec((1, n_ctx, QK_PAD), lambda h, i: (h // k_rep, 0, 0)),
        pl.BlockSpec((1, V_ROWS, n_ctx), lambda h, i: (h // v_rep, 0, 0)),
    ]
    if has_lat:
        args += [k_lat, vt_lat]
        in_specs += [
            pl.BlockSpec((1, n_lat, QK_PAD), lambda h, i: (h // k_rep, 0, 0)),
            pl.BlockSpec((1, V_ROWS, n_lat), lambda h, i: (h // v_rep, 0, 0)),
        ]
    return pl.pallas_call(
        functools.partial(_flash_kernel, n_lat_steps=n_lat // KV_TILE, has_sink=has_sink),
        grid=(n_heads, tq_total // tq),
        in_specs=in_specs,
        out_specs=pl.BlockSpec((HEAD_DIM, tq), lambda h, i: (h, i)),
        out_shape=jax.ShapeDtypeStruct((n_heads * HEAD_DIM, tq_total), BF16),
        scratch_shapes=[pltpu.VMEM((1, tq), F32), pltpu.VMEM((V_ROWS, tq), F32),
                        pltpu.VMEM((KV_TILE, tq), F32), pltpu.VMEM((KV_TILE, tq), F32)],
        compiler_params=_cparams(("arbitrary", "arbitrary")),
        name=name,
    )(*args)


def _window_kernel(sink_ref, qt_ref, kc_ref, vtc_ref, kp_ref, km_ref, kn_ref, vp_ref, vm_ref, vn_ref,
                   bias_ref, o_ref):
    i = pl.program_id(0)
    n = pl.num_programs(0)
    sink = sink_ref[pl.program_id(1)]
    q = qt_ref[0]
    dot = functools.partial(jnp.dot, preferred_element_type=F32)
    edge_prev = jnp.where(i == 0, NEG_INF, 0.0).astype(F32)
    edge_next = jnp.where(i == n - 1, NEG_INF, 0.0).astype(F32)
    s_c = dot(kc_ref[0], q)
    s_p = dot(kp_ref[0], q) + (bias_ref[0:HALO, :] + edge_prev)
    s_m = dot(km_ref[0], q) + bias_ref[HALO:HALO + Q_TILE, :]
    s_n = dot(kn_ref[0], q) + (bias_ref[HALO + Q_TILE:, :] + edge_next)
    m = jnp.maximum(jnp.maximum(jnp.max(s_c, axis=0, keepdims=True), jnp.max(s_p, axis=0, keepdims=True)),
                    jnp.maximum(jnp.max(s_m, axis=0, keepdims=True), jnp.max(s_n, axis=0, keepdims=True)))
    m = jnp.maximum(m, sink)
    pv = lambda v_ref, s: dot(v_ref[0], jnp.exp2(s - m).astype(BF16))
    acc = pv(vtc_ref, s_c) + pv(vp_ref, s_p) + pv(vm_ref, s_m) + pv(vn_ref, s_n)
    denom = acc[HEAD_DIM:HEAD_DIM + 1] + jnp.exp2(sink - m)
    o_ref[...] = (acc[0:HEAD_DIM] / denom).astype(o_ref.dtype)


def _window_bias():
    r = np.arange(Q_TILE + 2 * HALO)[:, None] - HALO
    c = np.arange(Q_TILE)[None, :]
    return np.where(np.abs(c - r) <= WINDOW, 0.0, NEG_INF).astype(np.float32)


def _window_attention(qt, k_ctx, vt_ctx, k_lat, vt_lat, sink):
    n_heads, _, t = qt.shape
    n_ctx = k_ctx.shape[1]
    v_rep = n_heads // vt_ctx.shape[0]
    per = Q_TILE // HALO
    last = t // HALO - 1
    prev_blk = lambda i: jnp.maximum(i * per - 1, 0)
    next_blk = lambda i: jnp.minimum(i * per + per, last)
    in_specs = [
        pl.BlockSpec(memory_space=pltpu.SMEM),
        pl.BlockSpec((1, QK_PAD, Q_TILE), lambda i, h: (h, 0, i)),
        pl.BlockSpec((1, n_ctx, QK_PAD), lambda i, h: (0, 0, 0)),
        pl.BlockSpec((1, V_ROWS, n_ctx), lambda i, h: (h // v_rep, 0, 0)),
        pl.BlockSpec((1, HALO, QK_PAD), lambda i, h: (0, prev_blk(i), 0)),
        pl.BlockSpec((1, Q_TILE, QK_PAD), lambda i, h: (0, i, 0)),
        pl.BlockSpec((1, HALO, QK_PAD), lambda i, h: (0, next_blk(i), 0)),
        pl.BlockSpec((1, V_ROWS, HALO), lambda i, h: (h // v_rep, 0, prev_blk(i))),
        pl.BlockSpec((1, V_ROWS, Q_TILE), lambda i, h: (h // v_rep, 0, i)),
        pl.BlockSpec((1, V_ROWS, HALO), lambda i, h: (h // v_rep, 0, next_blk(i))),
        pl.BlockSpec((Q_TILE + 2 * HALO, Q_TILE), lambda i, h: (0, 0)),
    ]
    return pl.pallas_call(
        _window_kernel,
        grid=(t // Q_TILE, n_heads),
        in_specs=in_specs,
        out_specs=pl.BlockSpec((HEAD_DIM, Q_TILE), lambda i, h: (h, i)),
        out_shape=jax.ShapeDtypeStruct((n_heads * HEAD_DIM, t), BF16),
        compiler_params=_cparams(("arbitrary", "arbitrary")),
        name="window_attn",
    )(sink, qt, k_ctx, vt_ctx, k_lat, k_lat, k_lat, vt_lat, vt_lat, vt_lat, jnp.asarray(_window_bias()))


def _pool_kernel(u_ref, up_ref, un_ref, band_ref, w_ref, scale_ref, o_ref, *, t_total):
    i = pl.program_id(0)
    n = pl.num_programs(0)
    r = u_ref.shape[0]
    u = u_ref[...]
    prev = jnp.where(i > 0, up_ref[...], 0.0)
    nxt = jnp.where(i < n - 1, un_ref[...], 0.0)
    ext_hi, ext_lo = _split_bf16(jnp.concatenate([prev, u, nxt], axis=0))
    lane = lax.broadcasted_iota(jnp.int32, (r, GROUP_WIDTH), 1)
    pos = lax.broadcasted_iota(jnp.int32, (r, GROUP_WIDTH), 0) + i * r
    total = jnp.zeros((r, GROUP_WIDTH), F32)
    half = jnp.zeros((r, GROUP_WIDTH), jnp.int32)
    for gi, size in enumerate(POOL_SIZES):
        band = band_ref[gi]
        w = jnp.dot(band, ext_hi, preferred_element_type=F32) + jnp.dot(band, ext_lo, preferred_element_type=F32)
        in_group = (lane >= gi * POOL_GROUP) & (lane < (gi + 1) * POOL_GROUP)
        total = jnp.where(in_group, w, total)
        half = jnp.where(in_group, size // 2, half)
    count = jnp.minimum(pos + half, t_total) - jnp.maximum(pos - half, 0)
    y = total / count.astype(F32) - u
    o_ref[...] = (jnp.dot(y.astype(BF16), w_ref[0], preferred_element_type=F32) * scale_ref[0]).astype(o_ref.dtype)


def _pool_bands():
    i = np.arange(ROW_TILE)[:, None]
    j = np.arange(ROW_TILE + 2 * HALO)[None, :] - HALO
    return np.stack([((j >= i - s // 2) & (j < i + s // 2)) for s in POOL_SIZES]).astype(np.float32)


def _pool(u, layer, p):
    t = u.shape[0]
    r = ROW_TILE
    per = r // HALO
    last = t // HALO - 1
    return pl.pallas_call(
        functools.partial(_pool_kernel, t_total=t),
        grid=(t // r,),
        in_specs=[
            pl.BlockSpec((r, GROUP_WIDTH), lambda i: (i, 0)),
            pl.BlockSpec((HALO, GROUP_WIDTH), lambda i: (jnp.maximum(i * per - 1, 0), 0)),
            pl.BlockSpec((HALO, GROUP_WIDTH), lambda i: (jnp.minimum(i * per + per, last), 0)),
            pl.BlockSpec((len(POOL_SIZES), r, r + 2 * HALO), lambda i: (0, 0, 0)),
            pl.BlockSpec((1, GROUP_WIDTH, GROUP_WIDTH), lambda i: (layer, 0, 0)),
            pl.BlockSpec((1, 1, GROUP_WIDTH), lambda i: (layer, 0, 0)),
        ],
        out_specs=pl.BlockSpec((r, GROUP_WIDTH), lambda i: (i, 0)),
        out_shape=jax.ShapeDtypeStruct((t, GROUP_WIDTH), BF16),
        compiler_params=_cparams(("arbitrary",)),
        name="pool",
    )(u, u, u, jnp.asarray(_pool_bands(), BF16), p["pool_bd"], p["pool_scale"])


def _outproj_kernel(x_ref, ota_ref, otc_ref, otd_ref, yb_ref, w_ref, mod_ref, gpost_ref, o_ref):
    ot = jnp.concatenate([ota_ref[...], otc_ref[...], otd_ref[...]], axis=0)
    n_att = ot.shape[0]
    y = lax.dot_general(ot, w_ref[0, 0:n_att, :], (((0,), (0,)), ((), ())), preferred_element_type=F32)
    y += jnp.dot(yb_ref[...], w_ref[0, n_att:, :], preferred_element_type=F32)
    o_ref[...] = x_ref[...] + mod_ref[0, 0, 2:3, :] * _rms(y, gpost_ref[0])


def _out_project(x, ot_a, ot_c, ot_d, yb, mod, stream, layer, p):
    t = x.shape[0]
    r = ROW_TILE
    feat = pl.BlockSpec((GROUP_WIDTH, r), lambda i: (0, i))
    return pl.pallas_call(
        _outproj_kernel,
        grid=(t // r,),
        in_specs=[
            pl.BlockSpec((r, D_MODEL), lambda i: (i, 0)),
            feat, feat, feat,
            pl.BlockSpec((r, GROUP_WIDTH), lambda i: (i, 0)),
            pl.BlockSpec((1, D_MODEL, D_MODEL), lambda i: (layer, 0, 0)),
            pl.BlockSpec((1, 1, N_MOD, D_MODEL), lambda i: (layer, stream, 0, 0)),
            pl.BlockSpec((1, 1, D_MODEL), lambda i: (layer, 0, 0)),
        ],
        out_specs=pl.BlockSpec((r, D_MODEL), lambda i: (i, 0)),
        out_shape=jax.ShapeDtypeStruct((t, D_MODEL), F32),
        compiler_params=_cparams(("arbitrary",)),
        name="out_proj",
    )(x, ot_a, ot_c, ot_d, yb, p["w_out"], mod, p["g_post1"])


def _ffn_kernel(x_ref, mod_ref, gpre_ref, wgu_ref, wdown_ref, gpost_ref, o_ref):
    x = x_ref[...]
    h = _rms(x, gpre_ref[0])
    h = h * (1.0 + mod_ref[0, 0, 4:5, :]) + mod_ref[0, 0, 3:4, :]
    gu = jnp.dot(h.astype(BF16), wgu_ref[0], preferred_element_type=F32)
    gate, up = gu[:, :D_FF], gu[:, D_FF:]
    act = (gate * jax.nn.sigmoid(gate) * up).astype(BF16)
    y = jnp.dot(act, wdown_ref[0], preferred_element_type=F32)
    o_ref[...] = x + mod_ref[0, 0, 5:6, :] * _rms(y, gpost_ref[0])


def _ffn(x, mod, stream, layer, p):
    t = x.shape[0]
    r = ROW_TILE
    return pl.pallas_call(
        _ffn_kernel,
        grid=(t // r,),
        in_specs=[
            pl.BlockSpec((r, D_MODEL), lambda i: (i, 0)),
            pl.BlockSpec((1, 1, N_MOD, D_MODEL), lambda i: (layer, stream, 0, 0)),
            pl.BlockSpec((1, 1, D_MODEL), lambda i: (layer, 0, 0)),
            pl.BlockSpec((1, D_MODEL, 2 * D_FF), lambda i: (layer, 0, 0)),
            pl.BlockSpec((1, D_FF, D_MODEL), lambda i: (layer, 0, 0)),
            pl.BlockSpec((1, 1, D_MODEL), lambda i: (layer, 0, 0)),
        ],
        out_specs=pl.BlockSpec((r, D_MODEL), lambda i: (i, 0)),
        out_shape=jax.ShapeDtypeStruct((t, D_MODEL), F32),
        compiler_params=_cparams(("arbitrary",)),
        name="swiglu",
    )(x, mod, p["g_pre2"], p["w_gu"], p["w_down"], p["g_post2"])


def _rope_tables(n):
    rows = n // GRID_W
    row = jnp.repeat(jnp.arange(rows, dtype=F32), GRID_W)
    col = jnp.tile(jnp.arange(GRID_W, dtype=F32), rows)
    out = []
    for rot_dim in (HEAD_DIM, C_ROPE):
        n_axis = rot_dim // 4
        freqs = ROPE_THETA ** (-jnp.arange(n_axis, dtype=F32) / n_axis)
        ang = jnp.concatenate([row[:, None] * freqs, col[:, None] * freqs], axis=-1)
        out += [jnp.cos(ang).T, jnp.sin(ang).T]
    return tuple(out)


def _identity_rope(n):
    return (jnp.ones((HEAD_DIM // 2, n), F32), jnp.zeros((HEAD_DIM // 2, n), F32),
            jnp.ones((C_ROPE // 2, n), F32), jnp.zeros((C_ROPE // 2, n), F32))


def _pack_params(w_in, pool_w, w_out, w_gu, w_down, c_wuq, c_wukv):
    depth = w_in.shape[0]
    pad = jnp.zeros((depth, D_MODEL, IN_PACKED - IN_WIDTH), w_in.dtype)
    w_in_p = jnp.concatenate([w_in[:, :, :KR_END], pad, w_in[:, :, KR_END:]], axis=2).astype(BF16)
    eye = jnp.eye(len(POOL_SIZES), dtype=pool_w.dtype)
    pool_bd = jnp.einsum("lgce,gh->lgche", pool_w, eye).reshape(depth, GROUP_WIDTH, GROUP_WIDTH).astype(BF16)
    w_out_p = jnp.concatenate([w_out[:, 0:256], w_out[:, 512:1024], w_out[:, 256:512]], axis=1).astype(BF16)
    return dict(w_in=w_in_p, pool_bd=pool_bd, w_out=w_out_p, w_gu=w_gu.astype(BF16), w_down=w_down.astype(BF16),
                c_wuq=c_wuq.astype(BF16), c_wukv=c_wukv.astype(BF16))


def kernel(x, c, ctx, c_ctx, w_ada, b_ada, g_pre1, g_post1, w_in, a_sink, pool_w, pool_scale, c_gq, c_wuq, c_gkv,
           c_wukv, d_gq, d_gk, w_out, g_pre2, g_post2, w_gu, w_down):
    assert x.shape[0] == 1 and ctx.shape[0] == 1
    depth = w_in.shape[0]
    n = x.shape[1]
    n_ctx = ctx.shape[1]
    p = _pack_params(w_in, pool_w, w_out, w_gu, w_down, c_wuq, c_wukv)
    row3 = lambda a: a.reshape(depth, 1, a.shape[1])
    p.update(g_pre1=row3(g_pre1), g_post1=row3(g_post1), g_pre2=row3(g_pre2), g_post2=row3(g_post2),
             pool_scale=row3(pool_scale), c_gq=row3(c_gq), c_gkv=row3(c_gkv),
             d_gq=d_gq.reshape(depth, HEAD_DIM, 1), d_gk=d_gk.reshape(depth, HEAD_DIM, 1))

    cond = jnp.zeros((8, D_MODEL), F32).at[0].set(c[0]).at[1].set(c_ctx)
    mod = _modulation(cond, w_ada, b_ada).reshape(depth, 8, N_MOD, D_MODEL)

    rope_x = _rope_tables(n)
    rope_c = _identity_rope(n_ctx)
    xs, cs = x[0], ctx[0]
    for layer in range(depth):
        last = layer == depth - 1
        sink = a_sink[layer] * LOG2E
        qta, ka, vta, ub, qtc, kc, vtc, qtd, kd, vtd = _project(xs, mod, 0, layer, p, rope_x)
        qta_c, ka_c, vta_c, ub_c, qtc_c, kc_c, vtc_c, qtd_c, kd_c, vtd_c = _project(cs, mod, 1, layer, p, rope_c)

        ot_a = _window_attention(qta, ka_c, vta_c, ka, vta, sink)
        yb = _pool(ub, layer, p)
        ot_c = _dense_attention(qtc, kc_c, vtc_c, kc, vtc, name="dense_attn_c")
        ot_d = _dense_attention(qtd, kd_c, vtd_c, kd, vtd, name="dense_attn_d")
        xs = _out_project(xs, ot_a, ot_c, ot_d, yb, mod, 0, layer, p)

        if not last:
            ot_a_c = _dense_attention(qta_c, ka_c, vta_c, sink=sink, name="ctx_attn_a")
            yb_c = _pool(ub_c, layer, p)
            ot_c_c = _dense_attention(qtc_c, kc_c, vtc_c, name="ctx_attn_c")
            ot_d_c = _dense_attention(qtd_c, kd_c, vtd_c, name="ctx_attn_d")
            cs = _out_project(cs, ot_a_c, ot_c_c, ot_d_c, yb_c, mod, 1, layer, p)

        xs = _ffn(xs, mod, 0, layer, p)
        if not last:
            cs = _ffn(cs, mod, 1, layer, p)
    return xs[None]
```

```python
import functools

import numpy as np
import jax
import jax.numpy as jnp
from jax import lax
from jax.experimental import pallas as pl
from jax.experimental.pallas import tpu as pltpu

F32 = jnp.float32
BF16 = jnp.bfloat16

D_MODEL = 1024
GRID_W = 64
GROUP_WIDTH = 256
HEAD_DIM = 64
ROPE_THETA = 10000.0
EPS = 1e-6
NEG_INF = -1e30
LOG2E = 1.4426950408889634
N_MOD = 6
N_HEADS = 4
KV_HEADS = 2
WINDOW = 128
POOL_SIZES = (2, 4, 8, 16)
POOL_GROUP = 64
C_NOPE, C_ROPE, C_V = 64, 32, 64
C_Q_RANK, C_KV_RANK = 256, 128
D_FF = 2816
IN_WIDTH = 1696
KR_END = 1184
IN_PACKED = 1792

LANE = 128
QK_PAD = 128
V_ROWS = 80
ROW_TILE = 256
Q_TILE = 512
KV_BLOCK = 512
MXU_DEPTH = 256
MAX_TRIP_BLOCKS = 6
HALO = 128
POOL_HALO = 8
VMEM_LIMIT = 56 * 1024 * 1024


def _cparams(sem):
    return pltpu.CompilerParams(dimension_semantics=sem, vmem_limit_bytes=VMEM_LIMIT)


def _rms(x, g):
    ms = jnp.mean(x * x, axis=-1, keepdims=True)
    return x * lax.rsqrt(ms + EPS) * g


def _split_bf16(x):
    hi = x.astype(BF16)
    lo = (x - hi.astype(F32)).astype(BF16)
    return hi, lo


def _mod_kernel(cond_ref, w_ref, b_ref, o_ref):
    c = cond_ref[...]
    s = c * jax.nn.sigmoid(c)
    s_hi, s_lo = _split_bf16(s)
    w_hi, w_lo = _split_bf16(w_ref[0])
    acc = jnp.dot(s_hi, w_hi, preferred_element_type=F32)
    acc += jnp.dot(s_hi, w_lo, preferred_element_type=F32)
    acc += jnp.dot(s_lo, w_hi, preferred_element_type=F32)
    o_ref[0] = acc + b_ref[0]


def _modulation(cond, w_ada, b_ada):
    depth, d, n = w_ada.shape
    tn = 1536
    return pl.pallas_call(
        _mod_kernel,
        grid=(depth, n // tn),
        in_specs=[
            pl.BlockSpec((8, d), lambda l, j: (0, 0)),
            pl.BlockSpec((1, d, tn), lambda l, j: (l, 0, j)),
            pl.BlockSpec((1, 1, tn), lambda l, j: (l, 0, j)),
        ],
        out_specs=pl.BlockSpec((1, 8, tn), lambda l, j: (l, 0, j)),
        out_shape=jax.ShapeDtypeStruct((depth, 8, n), F32),
        compiler_params=_cparams(("arbitrary", "arbitrary")),
        name="modulation",
    )(cond, w_ada, b_ada.reshape(depth, 1, n))


def _rope_t(xt, cos, sin):
    half = xt.shape[0] // 2
    x1, x2 = xt[:half], xt[half:]
    return jnp.concatenate([x1 * cos - x2 * sin, x2 * cos + x1 * sin], axis=0)


def _ones_rows(r):
    row = lax.broadcasted_iota(jnp.int32, (V_ROWS - HEAD_DIM, r), 0)
    return jnp.where(row == 0, 1.0, 0.0).astype(BF16)


def _proj_kernel(x_ref, mod_ref, gpre_ref, win_ref, cgq_ref, cwuq_ref, cgkv_ref, cwukv_ref,
                 dgq_ref, dgk_ref, cos_hd_ref, sin_hd_ref, cos_c_ref, sin_c_ref,
                 qta_ref, ka_ref, vta_ref, u_ref, qtc_ref, kc_ref, vtc_ref, qtd_ref, kd_ref, vtd_ref):
    r = x_ref.shape[0]
    scale_hd = HEAD_DIM ** -0.5 * LOG2E
    scale_c = (C_NOPE + C_ROPE) ** -0.5 * LOG2E
    h = _rms(x_ref[...], gpre_ref[0])
    h = h * (1.0 + mod_ref[0, 0, 1:2, :]) + mod_ref[0, 0, 0:1, :]
    z = jnp.dot(h.astype(BF16), win_ref[0], preferred_element_type=F32)

    cos_hd, sin_hd = cos_hd_ref[...], sin_hd_ref[...]
    cos_c, sin_c = cos_c_ref[...], sin_c_ref[...]
    ones = _ones_rows(r)
    zeros64 = jnp.zeros((HEAD_DIM, r), BF16)

    def put_gqa_q(ref, qt, hd):
        g = hd // (N_HEADS // KV_HEADS)
        ref[hd, g * HEAD_DIM:(g + 1) * HEAD_DIM, :] = (qt * scale_hd).astype(BF16)
        ref[hd, (1 - g) * HEAD_DIM:(2 - g) * HEAD_DIM, :] = zeros64

    def put_v(ref, hd, vt):
        ref[hd, 0:HEAD_DIM, :] = vt.astype(BF16)
        ref[hd, HEAD_DIM:V_ROWS, :] = ones

    qa_t = z[:, 0:256].T
    for hd in range(N_HEADS):
        put_gqa_q(qta_ref, _rope_t(qa_t[hd * 64:(hd + 1) * 64], cos_hd, sin_hd), hd)
    ka_t = z[:, 256:384].T
    ka_t = jnp.concatenate([_rope_t(ka_t[g * 64:(g + 1) * 64], cos_hd, sin_hd) for g in range(KV_HEADS)], axis=0)
    ka_ref[0] = ka_t.T.astype(BF16)
    va_t = z[:, 384:512].T
    for g in range(KV_HEADS):
        put_v(vta_ref, g, va_t[g * 64:(g + 1) * 64])

    u_ref[...] = z[:, 512:768]

    cq = _rms(z[:, 768:1024], cgq_ref[0])
    qc_t = jnp.dot(cq.astype(BF16), cwuq_ref[0], preferred_element_type=F32).T
    ckv = _rms(z[:, 1024:1152], cgkv_ref[0])
    kvc_t = jnp.dot(ckv.astype(BF16), cwukv_ref[0], preferred_element_type=F32).T
    kr_t = _rope_t(z[:, 1152:1280].T[0:C_ROPE], cos_c, sin_c)
    pad32 = jnp.zeros((QK_PAD - C_NOPE - C_ROPE, r), F32)
    per_head_q = C_NOPE + C_ROPE
    for hd in range(N_HEADS):
        q_nope = qc_t[hd * per_head_q:hd * per_head_q + C_NOPE]
        q_rope = _rope_t(qc_t[hd * per_head_q + C_NOPE:(hd + 1) * per_head_q], cos_c, sin_c)
        qtc_ref[hd] = (jnp.concatenate([q_nope, q_rope, pad32], axis=0) * scale_c).astype(BF16)
        k_nope = kvc_t[hd * 128:hd * 128 + C_NOPE]
        kc_ref[hd] = jnp.concatenate([k_nope, kr_t, pad32], axis=0).T.astype(BF16)
        put_v(vtc_ref, hd, kvc_t[hd * 128 + C_NOPE:(hd + 1) * 128])

    def head_norm(xt, g_col):
        ms = jnp.mean(xt * xt, axis=0, keepdims=True)
        return xt * lax.rsqrt(ms + EPS) * g_col

    qd_t = z[:, 1280:1536].T
    for hd in range(N_HEADS):
        q = head_norm(qd_t[hd * 64:(hd + 1) * 64], dgq_ref[0])
        put_gqa_q(qtd_ref, _rope_t(q, cos_hd, sin_hd), hd)
    kd_t = z[:, 1536:1664].T
    kd_t = jnp.concatenate(
        [_rope_t(head_norm(kd_t[g * 64:(g + 1) * 64], dgk_ref[0]), cos_hd, sin_hd) for g in range(KV_HEADS)], axis=0)
    kd_ref[0] = kd_t.T.astype(BF16)
    vd_t = z[:, 1664:1792].T
    for g in range(KV_HEADS):
        put_v(vtd_ref, g, vd_t[g * 64:(g + 1) * 64])


def _project(x, mod, stream, layer, p, rope):
    t = x.shape[0]
    r = ROW_TILE
    cos_hd, sin_hd, cos_c, sin_c = rope
    lay3 = lambda i: (layer, 0, 0)
    tok = lambda i: (i, 0)
    feat3 = lambda i: (0, 0, i)
    tok3 = lambda i: (0, i, 0)
    out_shapes = (
        jax.ShapeDtypeStruct((N_HEADS, QK_PAD, t), BF16),
        jax.ShapeDtypeStruct((1, t, QK_PAD), BF16),
        jax.ShapeDtypeStruct((KV_HEADS, V_ROWS, t), BF16),
        jax.ShapeDtypeStruct((t, GROUP_WIDTH), F32),
        jax.ShapeDtypeStruct((N_HEADS, QK_PAD, t), BF16),
        jax.ShapeDtypeStruct((N_HEADS, t, QK_PAD), BF16),
        jax.ShapeDtypeStruct((N_HEADS, V_ROWS, t), BF16),
        jax.ShapeDtypeStruct((N_HEADS, QK_PAD, t), BF16),
        jax.ShapeDtypeStruct((1, t, QK_PAD), BF16),
        jax.ShapeDtypeStruct((KV_HEADS, V_ROWS, t), BF16),
    )
    out_specs = (
        pl.BlockSpec((N_HEADS, QK_PAD, r), feat3),
        pl.BlockSpec((1, r, QK_PAD), tok3),
        pl.BlockSpec((KV_HEADS, V_ROWS, r), feat3),
        pl.BlockSpec((r, GROUP_WIDTH), tok),
        pl.BlockSpec((N_HEADS, QK_PAD, r), feat3),
        pl.BlockSpec((N_HEADS, r, QK_PAD), tok3),
        pl.BlockSpec((N_HEADS, V_ROWS, r), feat3),
        pl.BlockSpec((N_HEADS, QK_PAD, r), feat3),
        pl.BlockSpec((1, r, QK_PAD), tok3),
        pl.BlockSpec((KV_HEADS, V_ROWS, r), feat3),
    )
    in_specs = [
        pl.BlockSpec((r, D_MODEL), tok),
        pl.BlockSpec((1, 1, N_MOD, D_MODEL), lambda i: (layer, stream, 0, 0)),
        pl.BlockSpec((1, 1, D_MODEL), lay3),
        pl.BlockSpec((1, D_MODEL, IN_PACKED), lay3),
        pl.BlockSpec((1, 1, C_Q_RANK), lay3),
        pl.BlockSpec((1, C_Q_RANK, p["c_wuq"].shape[2]), lay3),
        pl.BlockSpec((1, 1, C_KV_RANK), lay3),
        pl.BlockSpec((1, C_KV_RANK, p["c_wukv"].shape[2]), lay3),
        pl.BlockSpec((1, HEAD_DIM, 1), lay3),
        pl.BlockSpec((1, HEAD_DIM, 1), lay3),
        pl.BlockSpec((HEAD_DIM // 2, r), lambda i: (0, i)),
        pl.BlockSpec((HEAD_DIM // 2, r), lambda i: (0, i)),
        pl.BlockSpec((C_ROPE // 2, r), lambda i: (0, i)),
        pl.BlockSpec((C_ROPE // 2, r), lambda i: (0, i)),
    ]
    return pl.pallas_call(
        _proj_kernel,
        grid=(t // r,),
        in_specs=in_specs,
        out_specs=out_specs,
        out_shape=out_shapes,
        compiler_params=_cparams(("arbitrary",)),
        name="in_proj",
    )(x, mod, p["g_pre1"], p["w_in"], p["c_gq"], p["c_wuq"], p["c_gkv"], p["c_wukv"],
      p["d_gq"], p["d_gk"], cos_hd, sin_hd, cos_c, sin_c)


def _trip_blocks(n_blocks):
    return max(b for b in range(2, MAX_TRIP_BLOCKS + 1, 2) if (n_blocks - 2) % b == 0)


def _flash_kernel(*refs, n_blocks, has_sink):
    refs = list(refs)
    sink_ref = refs.pop(0) if has_sink else None
    qt_ref, kc_ref, vtc_ref = refs[:3]
    refs = refs[3:]
    if n_blocks:
        kl_ref, vtl_ref = refs[:2]
        refs = refs[2:]
    o_ref, m_ref, acc_ref = refs[:3]
    scratch = refs[3:]

    q = qt_ref[0]
    s = jnp.dot(kc_ref[0], q, preferred_element_type=F32)

    if n_blocks:
        s_refs, cm_refs, p_refs, al_refs = scratch[0:2], scratch[2:4], scratch[4:6], scratch[6:8]
        halves = KV_BLOCK // MXU_DEPTH

        def produce_part(slot, j, h, cm):
            off = pl.multiple_of(j * KV_BLOCK + h * MXU_DEPTH, MXU_DEPTH)
            sj = jnp.dot(kl_ref[0, pl.ds(off, MXU_DEPTH), :], q, preferred_element_type=F32)
            s_refs[slot][h * MXU_DEPTH:(h + 1) * MXU_DEPTH, :] = sj
            cmh = jnp.max(sj, axis=0, keepdims=True)
            return cmh if cm is None else jnp.maximum(cm, cmh)

        def soften(slot):
            m_old = m_ref[...]
            m_new = jnp.maximum(m_old, cm_refs[slot][...])
            al_refs[slot][...] = jnp.exp2(m_old - m_new)
            p_refs[slot][...] = jnp.exp2(s_refs[slot][...] - m_new).astype(BF16)
            m_ref[...] = m_new

        def accumulate_part(slot, j, h):
            off = pl.multiple_of(j * KV_BLOCK + h * MXU_DEPTH, MXU_DEPTH)
            pv = jnp.dot(vtl_ref[0, :, pl.ds(off, MXU_DEPTH)], p_refs[slot][h * MXU_DEPTH:(h + 1) * MXU_DEPTH, :],
                         preferred_element_type=F32)
            scale = al_refs[slot][...] if h == 0 else 1.0
            acc_ref[...] = scale * acc_ref[...] + pv

        def stages(slot_p, j_p, slot_a, j_a):
            cm = None
            for h in range(halves):
                if j_p is not None:
                    cm = produce_part(slot_p, j_p, h, cm)
                if j_a is not None:
                    accumulate_part(slot_a, j_a, h)
            if j_p is not None:
                cm_refs[slot_p][...] = cm

        stages(0, 0, None, None)
        stages(1, 1, None, None)

    m = jnp.max(s, axis=0, keepdims=True)
    if has_sink:
        sink = sink_ref[pl.program_id(0)]
        m = jnp.maximum(m, sink)
    p = jnp.exp2(s - m)
    acc = jnp.dot(vtc_ref[0], p.astype(BF16), preferred_element_type=F32)
    if has_sink:
        row = lax.broadcasted_iota(jnp.int32, acc.shape, 0)
        acc = acc + jnp.where(row == HEAD_DIM, jnp.exp2(sink - m), 0.0)

    if n_blocks:
        m_ref[...] = m
        acc_ref[...] = acc
        p_refs[1][...] = jnp.zeros(p_refs[1].shape, BF16)
        al_refs[1][...] = jnp.ones(al_refs[1].shape, F32)
        per_trip = _trip_blocks(n_blocks)

        def trip(t, carry):
            for b in range(per_trip):
                j = t * per_trip + b
                soften(b % 2)
                stages(b % 2, j + 2, 1 - b % 2, jnp.maximum(j - 1, 0))
            return carry

        lax.fori_loop(0, (n_blocks - 2) // per_trip, trip, 0)
        for j in (n_blocks - 2, n_blocks - 1):
            soften(j % 2)
            stages(None, None, 1 - j % 2, j - 1)
        stages(None, None, (n_blocks - 1) % 2, n_blocks - 1)
        acc = acc_ref[...]
    o_ref[...] = (acc[0:HEAD_DIM] / acc[HEAD_DIM:HEAD_DIM + 1]).astype(o_ref.dtype)


def _dense_attention(qt, k_ctx, vt_ctx, k_lat=None, vt_lat=None, sink=None, name="dense_attn"):
    n_heads, _, tq_total = qt.shape
    tq = min(Q_TILE, tq_total)
    n_ctx = k_ctx.shape[1]
    k_rep = n_heads // k_ctx.shape[0]
    v_rep = n_heads // vt_ctx.shape[0]
    has_sink = sink is not None
    has_lat = k_lat is not None
    n_lat = k_lat.shape[1] if has_lat else 0
    n_blocks = n_lat // KV_BLOCK
    assert n_lat % (2 * KV_BLOCK) == 0 and n_blocks != 2
    args, in_specs = [], []
    if has_sink:
        args.append(sink)
        in_specs.append(pl.BlockSpec(memory_space=pltpu.SMEM))
    args += [qt, k_ctx, vt_ctx]
    in_specs += [
        pl.BlockSpec((1, QK_PAD, tq), lambda h, i: (h, 0, i)),
        pl.BlockSpec((1, n_ctx, QK_PAD), lambda h, i: (h // k_rep, 0, 0)),
        pl.BlockSpec((1, V_ROWS, n_ctx), lambda h, i: (h // v_rep, 0, 0)),
    ]
    scratch = [pltpu.VMEM((1, tq), F32), pltpu.VMEM((V_ROWS, tq), F32)]
    if has_lat:
        args += [k_lat, vt_lat]
        in_specs += [
            pl.BlockSpec((1, n_lat, QK_PAD), lambda h, i: (h // k_rep, 0, 0)),
            pl.BlockSpec((1, V_ROWS, n_lat), lambda h, i: (h // v_rep, 0, 0)),
        ]
        scratch += (2 * [pltpu.VMEM((KV_BLOCK, tq), F32)] + 2 * [pltpu.VMEM((1, tq), F32)]
                    + 2 * [pltpu.VMEM((KV_BLOCK, tq), BF16)] + 2 * [pltpu.VMEM((1, tq), F32)])
    return pl.pallas_call(
        functools.partial(_flash_kernel, n_blocks=n_blocks, has_sink=has_sink),
        grid=(n_heads, tq_total // tq),
        in_specs=in_specs,
        out_specs=pl.BlockSpec((HEAD_DIM, tq), lambda h, i: (h, i)),
        out_shape=jax.ShapeDtypeStruct((n_heads * HEAD_DIM, tq_total), BF16),
        scratch_shapes=scratch,
        compiler_params=_cparams(("arbitrary", "arbitrary")),
        name=name,
    )(*args)


def _window_kernel(sink_ref, qt_ref, kc_ref, vtc_ref, kp_ref, km_ref, kn_ref, vp_ref, vm_ref, vn_ref,
                   bias_ref, o_ref):
    i = pl.program_id(0)
    n = pl.num_programs(0)
    n_heads = qt_ref.shape[0]
    v_rep = n_heads // vtc_ref.shape[0]
    dot = functools.partial(jnp.dot, preferred_element_type=F32)
    edge_prev = jnp.where(i == 0, NEG_INF, 0.0).astype(F32)
    edge_next = jnp.where(i == n - 1, NEG_INF, 0.0).astype(F32)
    bias_p = bias_ref[0:HALO, :] + edge_prev
    bias_m = bias_ref[HALO:HALO + Q_TILE, :]
    bias_n = bias_ref[HALO + Q_TILE:, :] + edge_next

    def scores(hd):
        q = qt_ref[hd]
        return (dot(kc_ref[0], q), dot(kp_ref[0], q) + bias_p, dot(km_ref[0], q) + bias_m,
                dot(kn_ref[0], q) + bias_n)

    def finish(hd, ss):
        g = hd // v_rep
        sink = sink_ref[hd]
        m = functools.reduce(jnp.maximum, [jnp.max(s, axis=0, keepdims=True) for s in ss])
        m = jnp.maximum(m, sink)
        v_refs = (vtc_ref, vp_ref, vm_ref, vn_ref)
        acc = sum(dot(v_ref[g], jnp.exp2(s - m).astype(BF16)) for v_ref, s in zip(v_refs, ss))
        denom = acc[HEAD_DIM:HEAD_DIM + 1] + jnp.exp2(sink - m)
        o_ref[hd * HEAD_DIM:(hd + 1) * HEAD_DIM, :] = (acc[0:HEAD_DIM] / denom).astype(o_ref.dtype)

    ss = scores(0)
    for hd in range(n_heads):
        nxt = scores(hd + 1) if hd + 1 < n_heads else None
        finish(hd, ss)
        ss = nxt


def _window_bias():
    r = np.arange(Q_TILE + 2 * HALO)[:, None] - HALO
    c = np.arange(Q_TILE)[None, :]
    return np.where(np.abs(c - r) <= WINDOW, 0.0, NEG_INF).astype(np.float32)


def _window_attention(qt, k_ctx, vt_ctx, k_lat, vt_lat, sink):
    n_heads, _, t = qt.shape
    n_ctx = k_ctx.shape[1]
    n_kv = vt_ctx.shape[0]
    per = Q_TILE // HALO
    last = t // HALO - 1
    prev_blk = lambda i: jnp.maximum(i * per - 1, 0)
    next_blk = lambda i: jnp.minimum(i * per + per, last)
    in_specs = [
        pl.BlockSpec(memory_space=pltpu.SMEM),
        pl.BlockSpec((n_heads, QK_PAD, Q_TILE), lambda i: (0, 0, i)),
        pl.BlockSpec((1, n_ctx, QK_PAD), lambda i: (0, 0, 0)),
        pl.BlockSpec((n_kv, V_ROWS, n_ctx), lambda i: (0, 0, 0)),
        pl.BlockSpec((1, HALO, QK_PAD), lambda i: (0, prev_blk(i), 0)),
        pl.BlockSpec((1, Q_TILE, QK_PAD), lambda i: (0, i, 0)),
        pl.BlockSpec((1, HALO, QK_PAD), lambda i: (0, next_blk(i), 0)),
        pl.BlockSpec((n_kv, V_ROWS, HALO), lambda i: (0, 0, prev_blk(i))),
        pl.BlockSpec((n_kv, V_ROWS, Q_TILE), lambda i: (0, 0, i)),
        pl.BlockSpec((n_kv, V_ROWS, HALO), lambda i: (0, 0, next_blk(i))),
        pl.BlockSpec((Q_TILE + 2 * HALO, Q_TILE), lambda i: (0, 0)),
    ]
    return pl.pallas_call(
        _window_kernel,
        grid=(t // Q_TILE,),
        in_specs=in_specs,
        out_specs=pl.BlockSpec((n_heads * HEAD_DIM, Q_TILE), lambda i: (0, i)),
        out_shape=jax.ShapeDtypeStruct((n_heads * HEAD_DIM, t), BF16),
        compiler_params=_cparams(("arbitrary",)),
        name="window_attn",
    )(sink, qt, k_ctx, vt_ctx, k_lat, k_lat, k_lat, vt_lat, vt_lat, vt_lat, jnp.asarray(_window_bias()))


def _pool_kernel(u_ref, up_ref, un_ref, w_ref, scale_ref, o_ref, *, t_total):
    i = pl.program_id(0)
    n = pl.num_programs(0)
    r = u_ref.shape[0]
    u = u_ref[...]
    prev = jnp.where(i > 0, up_ref[...], 0.0)
    nxt = jnp.where(i < n - 1, un_ref[...], 0.0)
    ext = jnp.concatenate([prev, u, nxt], axis=0)
    rows = ext.shape[0]

    def shifted(x, k):
        return pltpu.roll(x, k % rows, axis=0)

    def window_sums(e):
        s2 = shifted(e, 1) + e
        s4 = shifted(s2, 1) + shifted(s2, -1)
        s8 = shifted(s4, 2) + shifted(s4, -2)
        s16 = shifted(s8, 4) + shifted(s8, -4)
        return s2, s4, s8, s16

    low_group = lax.broadcasted_iota(jnp.int32, (rows, LANE), 1) < POOL_GROUP
    s2, s4, _, _ = window_sums(ext[:, 0:LANE])
    _, _, s8, s16 = window_sums(ext[:, LANE:2 * LANE])
    total = jnp.concatenate([jnp.where(low_group, s2, s4), jnp.where(low_group, s8, s16)], axis=1)
    total = total[POOL_HALO:POOL_HALO + r]

    lane = lax.broadcasted_iota(jnp.int32, (r, GROUP_WIDTH), 1)
    pos = lax.broadcasted_iota(jnp.int32, (r, GROUP_WIDTH), 0) + i * r
    half = jnp.zeros((r, GROUP_WIDTH), jnp.int32)
    for gi, size in enumerate(POOL_SIZES):
        half = jnp.where((lane >= gi * POOL_GROUP) & (lane < (gi + 1) * POOL_GROUP), size // 2, half)
    count = jnp.minimum(pos + half, t_total) - jnp.maximum(pos - half, 0)
    y = total / count.astype(F32) - u
    o_ref[...] = (jnp.dot(y.astype(BF16), w_ref[0], preferred_element_type=F32) * scale_ref[0]).astype(o_ref.dtype)


def _pool(u, layer, p):
    t = u.shape[0]
    r = ROW_TILE
    per = r // POOL_HALO
    last = t // POOL_HALO - 1
    return pl.pallas_call(
        functools.partial(_pool_kernel, t_total=t),
        grid=(t // r,),
        in_specs=[
            pl.BlockSpec((r, GROUP_WIDTH), lambda i: (i, 0)),
            pl.BlockSpec((POOL_HALO, GROUP_WIDTH), lambda i: (jnp.maximum(i * per - 1, 0), 0)),
            pl.BlockSpec((POOL_HALO, GROUP_WIDTH), lambda i: (jnp.minimum(i * per + per, last), 0)),
            pl.BlockSpec((1, GROUP_WIDTH, GROUP_WIDTH), lambda i: (layer, 0, 0)),
            pl.BlockSpec((1, 1, GROUP_WIDTH), lambda i: (layer, 0, 0)),
        ],
        out_specs=pl.BlockSpec((r, GROUP_WIDTH), lambda i: (i, 0)),
        out_shape=jax.ShapeDtypeStruct((t, GROUP_WIDTH), BF16),
        compiler_params=_cparams(("arbitrary",)),
        name="pool",
    )(u, u, u, p["pool_bd"], p["pool_scale"])


def _mix_ffn_kernel(x_ref, ota_ref, otc_ref, otd_ref, yb_ref, wout_ref, mod_ref, gpost1_ref,
                    gpre2_ref, wgu_ref, wdown_ref, gpost2_ref, o_ref):
    ot = jnp.concatenate([ota_ref[...], otc_ref[...], otd_ref[...]], axis=0)
    n_att = ot.shape[0]
    y = lax.dot_general(ot, wout_ref[0, 0:n_att, :], (((0,), (0,)), ((), ())), preferred_element_type=F32)
    y += jnp.dot(yb_ref[...], wout_ref[0, n_att:, :], preferred_element_type=F32)
    x = x_ref[...] + mod_ref[0, 0, 2:3, :] * _rms(y, gpost1_ref[0])

    h = _rms(x, gpre2_ref[0])
    h = h * (1.0 + mod_ref[0, 0, 4:5, :]) + mod_ref[0, 0, 3:4, :]
    gu = jnp.dot(h.astype(BF16), wgu_ref[0], preferred_element_type=F32)
    gate, up = gu[:, :D_FF], gu[:, D_FF:]
    act = (gate * jax.nn.sigmoid(gate) * up).astype(BF16)
    y2 = jnp.dot(act, wdown_ref[0], preferred_element_type=F32)
    o_ref[...] = x + mod_ref[0, 0, 5:6, :] * _rms(y2, gpost2_ref[0])


def _mix_ffn(x, ot_a, ot_c, ot_d, yb, mod, stream, layer, p):
    t = x.shape[0]
    r = ROW_TILE
    feat = pl.BlockSpec((GROUP_WIDTH, r), lambda i: (0, i))
    lay3 = lambda i: (layer, 0, 0)
    return pl.pallas_call(
        _mix_ffn_kernel,
        grid=(t // r,),
        in_specs=[
            pl.BlockSpec((r, D_MODEL), lambda i: (i, 0)),
            feat, feat, feat,
            pl.BlockSpec((r, GROUP_WIDTH), lambda i: (i, 0)),
            pl.BlockSpec((1, D_MODEL, D_MODEL), lay3),
            pl.BlockSpec((1, 1, N_MOD, D_MODEL), lambda i: (layer, stream, 0, 0)),
            pl.BlockSpec((1, 1, D_MODEL), lay3),
            pl.BlockSpec((1, 1, D_MODEL), lay3),
            pl.BlockSpec((1, D_MODEL, 2 * D_FF), lay3),
            pl.BlockSpec((1, D_FF, D_MODEL), lay3),
            pl.BlockSpec((1, 1, D_MODEL), lay3),
        ],
        out_specs=pl.BlockSpec((r, D_MODEL), lambda i: (i, 0)),
        out_shape=jax.ShapeDtypeStruct((t, D_MODEL), F32),
        compiler_params=_cparams(("arbitrary",)),
        name="mix_ffn",
    )(x, ot_a, ot_c, ot_d, yb, p["w_out"], mod, p["g_post1"], p["g_pre2"], p["w_gu"], p["w_down"], p["g_post2"])


def _rope_tables(n):
    rows = n // GRID_W
    row = jnp.repeat(jnp.arange(rows, dtype=F32), GRID_W)
    col = jnp.tile(jnp.arange(GRID_W, dtype=F32), rows)
    out = []
    for rot_dim in (HEAD_DIM, C_ROPE):
        n_axis = rot_dim // 4
        freqs = ROPE_THETA ** (-jnp.arange(n_axis, dtype=F32) / n_axis)
        ang = jnp.concatenate([row[:, None] * freqs, col[:, None] * freqs], axis=-1)
        out += [jnp.cos(ang).T, jnp.sin(ang).T]
    return tuple(out)


def _identity_rope(n):
    return (jnp.ones((HEAD_DIM // 2, n), F32), jnp.zeros((HEAD_DIM // 2, n), F32),
            jnp.ones((C_ROPE // 2, n), F32), jnp.zeros((C_ROPE // 2, n), F32))


def _pack_params(w_in, pool_w, w_out, w_gu, w_down, c_wuq, c_wukv):
    depth = w_in.shape[0]
    pad = jnp.zeros((depth, D_MODEL, IN_PACKED - IN_WIDTH), w_in.dtype)
    w_in_p = jnp.concatenate([w_in[:, :, :KR_END], pad, w_in[:, :, KR_END:]], axis=2).astype(BF16)
    eye = jnp.eye(len(POOL_SIZES), dtype=pool_w.dtype)
    pool_bd = jnp.einsum("lgce,gh->lgche", pool_w, eye).reshape(depth, GROUP_WIDTH, GROUP_WIDTH).astype(BF16)
    w_out_p = jnp.concatenate([w_out[:, 0:256], w_out[:, 512:1024], w_out[:, 256:512]], axis=1).astype(BF16)
    return dict(w_in=w_in_p, pool_bd=pool_bd, w_out=w_out_p, w_gu=w_gu.astype(BF16), w_down=w_down.astype(BF16),
                c_wuq=c_wuq.astype(BF16), c_wukv=c_wukv.astype(BF16))


def kernel(x, c, ctx, c_ctx, w_ada, b_ada, g_pre1, g_post1, w_in, a_sink, pool_w, pool_scale, c_gq, c_wuq, c_gkv,
           c_wukv, d_gq, d_gk, w_out, g_pre2, g_post2, w_gu, w_down):
    assert x.shape[0] == 1 and ctx.shape[0] == 1
    depth = w_in.shape[0]
    n = x.shape[1]
    n_ctx = ctx.shape[1]
    p = _pack_params(w_in, pool_w, w_out, w_gu, w_down, c_wuq, c_wukv)
    row3 = lambda a: a.reshape(depth, 1, a.shape[1])
    p.update(g_pre1=row3(g_pre1), g_post1=row3(g_post1), g_pre2=row3(g_pre2), g_post2=row3(g_post2),
             pool_scale=row3(pool_scale), c_gq=row3(c_gq), c_gkv=row3(c_gkv),
             d_gq=d_gq.reshape(depth, HEAD_DIM, 1), d_gk=d_gk.reshape(depth, HEAD_DIM, 1))

    cond = jnp.zeros((8, D_MODEL), F32).at[0].set(c[0]).at[1].set(c_ctx)
    mod = _modulation(cond, w_ada, b_ada).reshape(depth, 8, N_MOD, D_MODEL)

    rope_x = _rope_tables(n)
    rope_c = _identity_rope(n_ctx)
    xs, cs = x[0], ctx[0]
    for layer in range(depth):
        last = layer == depth - 1
        sink = a_sink[layer] * LOG2E
        qta, ka, vta, ub, qtc, kc, vtc, qtd, kd, vtd = _project(xs, mod, 0, layer, p, rope_x)
        qta_c, ka_c, vta_c, ub_c, qtc_c, kc_c, vtc_c, qtd_c, kd_c, vtd_c = _project(cs, mod, 1, layer, p, rope_c)

        ot_a = _window_attention(qta, ka_c, vta_c, ka, vta, sink)
        yb = _pool(ub, layer, p)
        ot_c = _dense_attention(qtc, kc_c, vtc_c, kc, vtc, name="dense_attn_c")
        ot_d = _dense_attention(qtd, kd_c, vtd_c, kd, vtd, name="dense_attn_d")
        xs = _mix_ffn(xs, ot_a, ot_c, ot_d, yb, mod, 0, layer, p)

        if not last:
            ot_a_c = _dense_attention(qta_c, ka_c, vta_c, sink=sink, name="ctx_attn_a")
            yb_c = _pool(ub_c, layer, p)
            ot_c_c = _dense_attention(qtc_c, kc_c, vtc_c, name="ctx_attn_c")
            ot_d_c = _dense_attention(qtd_c, kd_c, vtd_c, name="ctx_attn_d")
            cs = _mix_ffn(cs, ot_a_c, ot_c_c, ot_d_c, yb_c, mod, 1, layer, p)
    return xs[None]
```

```python
import functools

import numpy as np
import jax
import jax.numpy as jnp
from jax import lax
from jax.experimental import pallas as pl
from jax.experimental.pallas import tpu as pltpu

F32 = jnp.float32
BF16 = jnp.bfloat16

D_MODEL = 1024
GRID_W = 64
GROUP_WIDTH = 256
HEAD_DIM = 64
ROPE_THETA = 10000.0
EPS = 1e-6
NEG_INF = -1e30
LOG2E = 1.4426950408889634
N_MOD = 6
N_HEADS = 4
KV_HEADS = 2
WINDOW = 128
POOL_SIZES = (2, 4, 8, 16)
POOL_GROUP = 64
C_NOPE, C_ROPE, C_V = 64, 32, 64
C_Q_RANK, C_KV_RANK = 256, 128
D_FF = 2816
IN_WIDTH = 1696
KR_END = 1184
IN_PACKED = 1792

LANE = 128
QK_PAD = 128
V_ROWS = 80
ROW_TILE = 512
SUB_TILE = 256
Q_TILE = 512
FLASH_Q_TILE = 1024
KV_BLOCK = 512
MXU_DEPTH = 256
MAX_TRIP_BLOCKS = 6
HALO = 128
POOL_HALO = 8
VMEM_LIMIT = 56 * 1024 * 1024


def _cparams(sem):
    return pltpu.CompilerParams(dimension_semantics=sem, vmem_limit_bytes=VMEM_LIMIT)


def _rms(x, g):
    ms = jnp.mean(x * x, axis=-1, keepdims=True)
    return x * lax.rsqrt(ms + EPS) * g


def _split_bf16(x):
    hi = x.astype(BF16)
    lo = (x - hi.astype(F32)).astype(BF16)
    return hi, lo


def _mod_kernel(cond_ref, w_ref, b_ref, o_ref):
    c = cond_ref[...]
    s = c * jax.nn.sigmoid(c)
    s_hi, s_lo = _split_bf16(s)
    w_hi, w_lo = _split_bf16(w_ref[0])
    acc = jnp.dot(s_hi, w_hi, preferred_element_type=F32)
    acc += jnp.dot(s_hi, w_lo, preferred_element_type=F32)
    acc += jnp.dot(s_lo, w_hi, preferred_element_type=F32)
    o_ref[0] = acc + b_ref[0]


def _modulation(cond, w_ada, b_ada):
    depth, d, n = w_ada.shape
    tn = 1536
    return pl.pallas_call(
        _mod_kernel,
        grid=(depth, n // tn),
        in_specs=[
            pl.BlockSpec((8, d), lambda l, j: (0, 0)),
            pl.BlockSpec((1, d, tn), lambda l, j: (l, 0, j)),
            pl.BlockSpec((1, 1, tn), lambda l, j: (l, 0, j)),
        ],
        out_specs=pl.BlockSpec((1, 8, tn), lambda l, j: (l, 0, j)),
        out_shape=jax.ShapeDtypeStruct((depth, 8, n), F32),
        compiler_params=_cparams(("arbitrary", "arbitrary")),
        name="modulation",
    )(cond, w_ada, b_ada.reshape(depth, 1, n))


def _rope_t(xt, cos, sin):
    half = xt.shape[0] // 2
    x1, x2 = xt[:half], xt[half:]
    return jnp.concatenate([x1 * cos - x2 * sin, x2 * cos + x1 * sin], axis=0)


def _ones_rows(r):
    row = lax.broadcasted_iota(jnp.int32, (V_ROWS - HEAD_DIM, r), 0)
    return jnp.where(row == 0, 1.0, 0.0).astype(BF16)


def _proj_kernel(x_ref, mod_ref, gpre_ref, win_ref, cgq_ref, cwuq_ref, cgkv_ref, cwukv_ref,
                 dgq_ref, dgk_ref, cos_hd_ref, sin_hd_ref, cos_c_ref, sin_c_ref,
                 qta_ref, ka_ref, vta_ref, u_ref, qtc_ref, kc_ref, vtc_ref, qtd_ref, kd_ref, vtd_ref):
    subs = [slice(k, k + SUB_TILE) for k in range(0, x_ref.shape[0], SUB_TILE)]
    zs = []
    for sl in subs:
        h = _rms(x_ref[sl, :], gpre_ref[0])
        h = h * (1.0 + mod_ref[0, 0, 1:2, :]) + mod_ref[0, 0, 0:1, :]
        zs.append(jnp.dot(h.astype(BF16), win_ref[0], preferred_element_type=F32))
    for sl, z in zip(subs, zs):
        _proj_groups(z, sl, cgq_ref, cwuq_ref, cgkv_ref, cwukv_ref, dgq_ref, dgk_ref,
                     cos_hd_ref, sin_hd_ref, cos_c_ref, sin_c_ref,
                     qta_ref, ka_ref, vta_ref, u_ref, qtc_ref, kc_ref, vtc_ref, qtd_ref, kd_ref, vtd_ref)


def _proj_groups(z, sl, cgq_ref, cwuq_ref, cgkv_ref, cwukv_ref, dgq_ref, dgk_ref,
                 cos_hd_ref, sin_hd_ref, cos_c_ref, sin_c_ref,
                 qta_ref, ka_ref, vta_ref, u_ref, qtc_ref, kc_ref, vtc_ref, qtd_ref, kd_ref, vtd_ref):
    r = z.shape[0]
    scale_hd = HEAD_DIM ** -0.5 * LOG2E
    scale_c = (C_NOPE + C_ROPE) ** -0.5 * LOG2E
    cos_hd, sin_hd = cos_hd_ref[:, sl], sin_hd_ref[:, sl]
    cos_c, sin_c = cos_c_ref[:, sl], sin_c_ref[:, sl]
    ones = _ones_rows(r)
    zeros64 = jnp.zeros((HEAD_DIM, r), BF16)

    def put_gqa_q(ref, qt, hd):
        g = hd // (N_HEADS // KV_HEADS)
        ref[hd, g * HEAD_DIM:(g + 1) * HEAD_DIM, sl] = (qt * scale_hd).astype(BF16)
        ref[hd, (1 - g) * HEAD_DIM:(2 - g) * HEAD_DIM, sl] = zeros64

    def put_v(ref, hd, vt):
        ref[hd, 0:HEAD_DIM, sl] = vt.astype(BF16)
        ref[hd, HEAD_DIM:V_ROWS, sl] = ones

    qa_t = z[:, 0:256].T
    for hd in range(N_HEADS):
        put_gqa_q(qta_ref, _rope_t(qa_t[hd * 64:(hd + 1) * 64], cos_hd, sin_hd), hd)
    ka_t = z[:, 256:384].T
    ka_t = jnp.concatenate([_rope_t(ka_t[g * 64:(g + 1) * 64], cos_hd, sin_hd) for g in range(KV_HEADS)], axis=0)
    ka_ref[0, sl, :] = ka_t.T.astype(BF16)
    va_t = z[:, 384:512].T
    for g in range(KV_HEADS):
        put_v(vta_ref, g, va_t[g * 64:(g + 1) * 64])

    u_ref[sl, :] = z[:, 512:768]

    cq = _rms(z[:, 768:1024], cgq_ref[0])
    qc_t = jnp.dot(cq.astype(BF16), cwuq_ref[0], preferred_element_type=F32).T
    ckv = _rms(z[:, 1024:1152], cgkv_ref[0])
    kvc_t = jnp.dot(ckv.astype(BF16), cwukv_ref[0], preferred_element_type=F32).T
    kr_t = _rope_t(z[:, 1152:1280].T[0:C_ROPE], cos_c, sin_c)
    pad32 = jnp.zeros((QK_PAD - C_NOPE - C_ROPE, r), F32)
    per_head_q = C_NOPE + C_ROPE
    for hd in range(N_HEADS):
        q_nope = qc_t[hd * per_head_q:hd * per_head_q + C_NOPE]
        q_rope = _rope_t(qc_t[hd * per_head_q + C_NOPE:(hd + 1) * per_head_q], cos_c, sin_c)
        qtc_ref[hd, :, sl] = (jnp.concatenate([q_nope, q_rope, pad32], axis=0) * scale_c).astype(BF16)
        k_nope = kvc_t[hd * 128:hd * 128 + C_NOPE]
        kc_ref[hd, sl, :] = jnp.concatenate([k_nope, kr_t, pad32], axis=0).T.astype(BF16)
        put_v(vtc_ref, hd, kvc_t[hd * 128 + C_NOPE:(hd + 1) * 128])

    def head_norm(xt, g_col):
        ms = jnp.mean(xt * xt, axis=0, keepdims=True)
        return xt * lax.rsqrt(ms + EPS) * g_col

    qd_t = z[:, 1280:1536].T
    for hd in range(N_HEADS):
        q = head_norm(qd_t[hd * 64:(hd + 1) * 64], dgq_ref[0])
        put_gqa_q(qtd_ref, _rope_t(q, cos_hd, sin_hd), hd)
    kd_t = z[:, 1536:1664].T
    kd_t = jnp.concatenate(
        [_rope_t(head_norm(kd_t[g * 64:(g + 1) * 64], dgk_ref[0]), cos_hd, sin_hd) for g in range(KV_HEADS)], axis=0)
    kd_ref[0, sl, :] = kd_t.T.astype(BF16)
    vd_t = z[:, 1664:1792].T
    for g in range(KV_HEADS):
        put_v(vtd_ref, g, vd_t[g * 64:(g + 1) * 64])


def _project(x, mod, stream, layer, p, rope):
    t = x.shape[0]
    r = min(ROW_TILE, t)
    cos_hd, sin_hd, cos_c, sin_c = rope
    lay3 = lambda i: (layer, 0, 0)
    tok = lambda i: (i, 0)
    feat3 = lambda i: (0, 0, i)
    tok3 = lambda i: (0, i, 0)
    out_shapes = (
        jax.ShapeDtypeStruct((N_HEADS, QK_PAD, t), BF16),
        jax.ShapeDtypeStruct((1, t, QK_PAD), BF16),
        jax.ShapeDtypeStruct((KV_HEADS, V_ROWS, t), BF16),
        jax.ShapeDtypeStruct((t, GROUP_WIDTH), F32),
        jax.ShapeDtypeStruct((N_HEADS, QK_PAD, t), BF16),
        jax.ShapeDtypeStruct((N_HEADS, t, QK_PAD), BF16),
        jax.ShapeDtypeStruct((N_HEADS, V_ROWS, t), BF16),
        jax.ShapeDtypeStruct((N_HEADS, QK_PAD, t), BF16),
        jax.ShapeDtypeStruct((1, t, QK_PAD), BF16),
        jax.ShapeDtypeStruct((KV_HEADS, V_ROWS, t), BF16),
    )
    out_specs = (
        pl.BlockSpec((N_HEADS, QK_PAD, r), feat3),
        pl.BlockSpec((1, r, QK_PAD), tok3),
        pl.BlockSpec((KV_HEADS, V_ROWS, r), feat3),
        pl.BlockSpec((r, GROUP_WIDTH), tok),
        pl.BlockSpec((N_HEADS, QK_PAD, r), feat3),
        pl.BlockSpec((N_HEADS, r, QK_PAD), tok3),
        pl.BlockSpec((N_HEADS, V_ROWS, r), feat3),
        pl.BlockSpec((N_HEADS, QK_PAD, r), feat3),
        pl.BlockSpec((1, r, QK_PAD), tok3),
        pl.BlockSpec((KV_HEADS, V_ROWS, r), feat3),
    )
    in_specs = [
        pl.BlockSpec((r, D_MODEL), tok),
        pl.BlockSpec((1, 1, N_MOD, D_MODEL), lambda i: (layer, stream, 0, 0)),
        pl.BlockSpec((1, 1, D_MODEL), lay3),
        pl.BlockSpec((1, D_MODEL, IN_PACKED), lay3),
        pl.BlockSpec((1, 1, C_Q_RANK), lay3),
        pl.BlockSpec((1, C_Q_RANK, p["c_wuq"].shape[2]), lay3),
        pl.BlockSpec((1, 1, C_KV_RANK), lay3),
        pl.BlockSpec((1, C_KV_RANK, p["c_wukv"].shape[2]), lay3),
        pl.BlockSpec((1, HEAD_DIM, 1), lay3),
        pl.BlockSpec((1, HEAD_DIM, 1), lay3),
        pl.BlockSpec((HEAD_DIM // 2, r), lambda i: (0, i)),
        pl.BlockSpec((HEAD_DIM // 2, r), lambda i: (0, i)),
        pl.BlockSpec((C_ROPE // 2, r), lambda i: (0, i)),
        pl.BlockSpec((C_ROPE // 2, r), lambda i: (0, i)),
    ]
    return pl.pallas_call(
        _proj_kernel,
        grid=(t // r,),
        in_specs=in_specs,
        out_specs=out_specs,
        out_shape=out_shapes,
        compiler_params=_cparams(("arbitrary",)),
        name="in_proj",
    )(x, mod, p["g_pre1"], p["w_in"], p["c_gq"], p["c_wuq"], p["c_gkv"], p["c_wukv"],
      p["d_gq"], p["d_gk"], cos_hd, sin_hd, cos_c, sin_c)


def _trip_blocks(n_blocks):
    return max(b for b in range(2, MAX_TRIP_BLOCKS + 1, 2) if (n_blocks - 2) % b == 0)


def _flash_kernel(*refs, n_blocks, has_sink):
    refs = list(refs)
    sink_ref = refs.pop(0) if has_sink else None
    qt_ref, kc_ref, vtc_ref = refs[:3]
    refs = refs[3:]
    if n_blocks:
        kl_ref, vtl_ref = refs[:2]
        refs = refs[2:]
    o_ref, m_ref, acc_ref = refs[:3]
    scratch = refs[3:]

    q = qt_ref[0]
    s = jnp.dot(kc_ref[0], q, preferred_element_type=F32)

    if n_blocks:
        s_refs, cm_refs, p_refs, al_refs = scratch[0:2], scratch[2:4], scratch[4:6], scratch[6:8]
        halves = KV_BLOCK // MXU_DEPTH

        def produce_part(slot, j, h, cm):
            off = pl.multiple_of(j * KV_BLOCK + h * MXU_DEPTH, MXU_DEPTH)
            sj = jnp.dot(kl_ref[0, pl.ds(off, MXU_DEPTH), :], q, preferred_element_type=F32)
            s_refs[slot][h * MXU_DEPTH:(h + 1) * MXU_DEPTH, :] = sj
            cmh = jnp.max(sj, axis=0, keepdims=True)
            return cmh if cm is None else jnp.maximum(cm, cmh)

        def soften(slot):
            m_old = m_ref[...]
            m_new = jnp.maximum(m_old, cm_refs[slot][...])
            al_refs[slot][...] = jnp.exp2(m_old - m_new)
            p_refs[slot][...] = jnp.exp2(s_refs[slot][...] - m_new).astype(BF16)
            m_ref[...] = m_new

        def accumulate_part(slot, j, h):
            off = pl.multiple_of(j * KV_BLOCK + h * MXU_DEPTH, MXU_DEPTH)
            pv = jnp.dot(vtl_ref[0, :, pl.ds(off, MXU_DEPTH)], p_refs[slot][h * MXU_DEPTH:(h + 1) * MXU_DEPTH, :],
                         preferred_element_type=F32)
            scale = al_refs[slot][...] if h == 0 else 1.0
            acc_ref[...] = scale * acc_ref[...] + pv

        def stages(slot_p, j_p, slot_a, j_a):
            cm = None
            for h in range(halves):
                if j_p is not None:
                    cm = produce_part(slot_p, j_p, h, cm)
                if j_a is not None:
                    accumulate_part(slot_a, j_a, h)
            if j_p is not None:
                cm_refs[slot_p][...] = cm

        stages(0, 0, None, None)

    m = jnp.max(s, axis=0, keepdims=True)
    if has_sink:
        sink = sink_ref[pl.program_id(0)]
        m = jnp.maximum(m, sink)
    p = jnp.exp2(s - m)
    acc = jnp.dot(vtc_ref[0], p.astype(BF16), preferred_element_type=F32)
    if has_sink:
        row = lax.broadcasted_iota(jnp.int32, acc.shape, 0)
        acc = acc + jnp.where(row == HEAD_DIM, jnp.exp2(sink - m), 0.0)

    if n_blocks:
        m_ref[...] = m
        acc_ref[...] = acc
        p_refs[1][...] = jnp.zeros(p_refs[1].shape, BF16)
        al_refs[1][...] = jnp.ones(al_refs[1].shape, F32)
        per_trip = _trip_blocks(n_blocks)
        stages(1, 1, None, None)
        soften(0)

        def trip(t, carry):
            for b in range(per_trip):
                j = t * per_trip + b
                stages(b % 2, j + 2, 1 - b % 2, jnp.maximum(j - 1, 0))
                soften(1 - b % 2)
            return carry

        lax.fori_loop(0, (n_blocks - 2) // per_trip, trip, 0)
        stages(None, None, 1, n_blocks - 3)
        soften(1)
        stages(None, None, 0, n_blocks - 2)
        stages(None, None, 1, n_blocks - 1)
        acc = acc_ref[...]
    o_ref[...] = (acc[0:HEAD_DIM] / acc[HEAD_DIM:HEAD_DIM + 1]).astype(o_ref.dtype)


def _dense_attention(qt, k_ctx, vt_ctx, k_lat=None, vt_lat=None, sink=None, name="dense_attn"):
    n_heads, _, tq_total = qt.shape
    tq = min(FLASH_Q_TILE, tq_total)
    n_ctx = k_ctx.shape[1]
    k_rep = n_heads // k_ctx.shape[0]
    v_rep = n_heads // vt_ctx.shape[0]
    has_sink = sink is not None
    has_lat = k_lat is not None
    n_lat = k_lat.shape[1] if has_lat else 0
    n_blocks = n_lat // KV_BLOCK
    assert n_lat % (2 * KV_BLOCK) == 0 and n_blocks != 2
    args, in_specs = [], []
    if has_sink:
        args.append(sink)
        in_specs.append(pl.BlockSpec(memory_space=pltpu.SMEM))
    args += [qt, k_ctx, vt_ctx]
    in_specs += [
        pl.BlockSpec((1, QK_PAD, tq), lambda h, i: (h, 0, i)),
        pl.BlockSpec((1, n_ctx, QK_PAD), lambda h, i: (h // k_rep, 0, 0)),
        pl.BlockSpec((1, V_ROWS, n_ctx), lambda h, i: (h // v_rep, 0, 0)),
    ]
    scratch = [pltpu.VMEM((1, tq), F32), pltpu.VMEM((V_ROWS, tq), F32)]
    if has_lat:
        args += [k_lat, vt_lat]
        in_specs += [
            pl.BlockSpec((1, n_lat, QK_PAD), lambda h, i: (h // k_rep, 0, 0)),
            pl.BlockSpec((1, V_ROWS, n_lat), lambda h, i: (h // v_rep, 0, 0)),
        ]
        scratch += (2 * [pltpu.VMEM((KV_BLOCK, tq), F32)] + 2 * [pltpu.VMEM((1, tq), F32)]
                    + 2 * [pltpu.VMEM((KV_BLOCK, tq), BF16)] + 2 * [pltpu.VMEM((1, tq), F32)])
    return pl.pallas_call(
        functools.partial(_flash_kernel, n_blocks=n_blocks, has_sink=has_sink),
        grid=(n_heads, tq_total // tq),
        in_specs=in_specs,
        out_specs=pl.BlockSpec((HEAD_DIM, tq), lambda h, i: (h, i)),
        out_shape=jax.ShapeDtypeStruct((n_heads * HEAD_DIM, tq_total), BF16),
        scratch_shapes=scratch,
        compiler_params=_cparams(("arbitrary", "arbitrary")),
        name=name,
    )(*args)


def _window_kernel(sink_ref, qt_ref, kc_ref, vtc_ref, kp_ref, km_ref, kn_ref, vp_ref, vm_ref, vn_ref,
                   bias_ref, o_ref):
    i = pl.program_id(0)
    n = pl.num_programs(0)
    n_heads = qt_ref.shape[0]
    v_rep = n_heads // vtc_ref.shape[0]
    dot = functools.partial(jnp.dot, preferred_element_type=F32)
    edge_prev = jnp.where(i == 0, NEG_INF, 0.0).astype(F32)
    edge_next = jnp.where(i == n - 1, NEG_INF, 0.0).astype(F32)
    bias_p = bias_ref[0:HALO, :] + edge_prev
    bias_m = bias_ref[HALO:HALO + Q_TILE, :]
    bias_n = bias_ref[HALO + Q_TILE:, :] + edge_next

    def scores(hd):
        q = qt_ref[hd]
        return (dot(kc_ref[0], q), dot(kp_ref[0], q) + bias_p, dot(km_ref[0], q) + bias_m,
                dot(kn_ref[0], q) + bias_n)

    def finish(hd, ss):
        g = hd // v_rep
        sink = sink_ref[hd]
        m = functools.reduce(jnp.maximum, [jnp.max(s, axis=0, keepdims=True) for s in ss])
        m = jnp.maximum(m, sink)
        v_refs = (vtc_ref, vp_ref, vm_ref, vn_ref)
        acc = sum(dot(v_ref[g], jnp.exp2(s - m).astype(BF16)) for v_ref, s in zip(v_refs, ss))
        denom = acc[HEAD_DIM:HEAD_DIM + 1] + jnp.exp2(sink - m)
        o_ref[hd * HEAD_DIM:(hd + 1) * HEAD_DIM, :] = (acc[0:HEAD_DIM] / denom).astype(o_ref.dtype)

    ss = scores(0)
    for hd in range(n_heads):
        nxt = scores(hd + 1) if hd + 1 < n_heads else None
        finish(hd, ss)
        ss = nxt


def _window_bias():
    r = np.arange(Q_TILE + 2 * HALO)[:, None] - HALO
    c = np.arange(Q_TILE)[None, :]
    return np.where(np.abs(c - r) <= WINDOW, 0.0, NEG_INF).astype(np.float32)


def _window_attention(qt, k_ctx, vt_ctx, k_lat, vt_lat, sink):
    n_heads, _, t = qt.shape
    n_ctx = k_ctx.shape[1]
    n_kv = vt_ctx.shape[0]
    per = Q_TILE // HALO
    last = t // HALO - 1
    prev_blk = lambda i: jnp.maximum(i * per - 1, 0)
    next_blk = lambda i: jnp.minimum(i * per + per, last)
    in_specs = [
        pl.BlockSpec(memory_space=pltpu.SMEM),
        pl.BlockSpec((n_heads, QK_PAD, Q_TILE), lambda i: (0, 0, i)),
        pl.BlockSpec((1, n_ctx, QK_PAD), lambda i: (0, 0, 0)),
        pl.BlockSpec((n_kv, V_ROWS, n_ctx), lambda i: (0, 0, 0)),
        pl.BlockSpec((1, HALO, QK_PAD), lambda i: (0, prev_blk(i), 0)),
        pl.BlockSpec((1, Q_TILE, QK_PAD), lambda i: (0, i, 0)),
        pl.BlockSpec((1, HALO, QK_PAD), lambda i: (0, next_blk(i), 0)),
        pl.BlockSpec((n_kv, V_ROWS, HALO), lambda i: (0, 0, prev_blk(i))),
        pl.BlockSpec((n_kv, V_ROWS, Q_TILE), lambda i: (0, 0, i)),
        pl.BlockSpec((n_kv, V_ROWS, HALO), lambda i: (0, 0, next_blk(i))),
        pl.BlockSpec((Q_TILE + 2 * HALO, Q_TILE), lambda i: (0, 0)),
    ]
    return pl.pallas_call(
        _window_kernel,
        grid=(t // Q_TILE,),
        in_specs=in_specs,
        out_specs=pl.BlockSpec((n_heads * HEAD_DIM, Q_TILE), lambda i: (0, i)),
        out_shape=jax.ShapeDtypeStruct((n_heads * HEAD_DIM, t), BF16),
        compiler_params=_cparams(("arbitrary",)),
        name="window_attn",
    )(sink, qt, k_ctx, vt_ctx, k_lat, k_lat, k_lat, vt_lat, vt_lat, vt_lat, jnp.asarray(_window_bias()))


def _pool_kernel(u_ref, up_ref, un_ref, w_ref, scale_ref, o_ref, *, t_total):
    i = pl.program_id(0)
    n = pl.num_programs(0)
    r = u_ref.shape[0]
    u = u_ref[...]
    prev = jnp.where(i > 0, up_ref[...], 0.0)
    nxt = jnp.where(i < n - 1, un_ref[...], 0.0)
    ext = jnp.concatenate([prev, u, nxt], axis=0)
    rows = ext.shape[0]

    def shifted(x, k):
        return pltpu.roll(x, k % rows, axis=0)

    def window_sums(e):
        s2 = shifted(e, 1) + e
        s4 = shifted(s2, 1) + shifted(s2, -1)
        s8 = shifted(s4, 2) + shifted(s4, -2)
        s16 = shifted(s8, 4) + shifted(s8, -4)
        return s2, s4, s8, s16

    low_group = lax.broadcasted_iota(jnp.int32, (rows, LANE), 1) < POOL_GROUP
    s2, s4, _, _ = window_sums(ext[:, 0:LANE])
    _, _, s8, s16 = window_sums(ext[:, LANE:2 * LANE])
    total = jnp.concatenate([jnp.where(low_group, s2, s4), jnp.where(low_group, s8, s16)], axis=1)
    total = total[POOL_HALO:POOL_HALO + r]

    lane = lax.broadcasted_iota(jnp.int32, (r, GROUP_WIDTH), 1)
    pos = lax.broadcasted_iota(jnp.int32, (r, GROUP_WIDTH), 0) + i * r
    half = jnp.zeros((r, GROUP_WIDTH), jnp.int32)
    for gi, size in enumerate(POOL_SIZES):
        half = jnp.where((lane >= gi * POOL_GROUP) & (lane < (gi + 1) * POOL_GROUP), size // 2, half)
    count = jnp.minimum(pos + half, t_total) - jnp.maximum(pos - half, 0)
    y = total / count.astype(F32) - u
    o_ref[...] = (jnp.dot(y.astype(BF16), w_ref[0], preferred_element_type=F32) * scale_ref[0]).astype(o_ref.dtype)


def _pool(u, layer, p):
    t = u.shape[0]
    r = min(ROW_TILE, t)
    per = r // POOL_HALO
    last = t // POOL_HALO - 1
    return pl.pallas_call(
        functools.partial(_pool_kernel, t_total=t),
        grid=(t // r,),
        in_specs=[
            pl.BlockSpec((r, GROUP_WIDTH), lambda i: (i, 0)),
            pl.BlockSpec((POOL_HALO, GROUP_WIDTH), lambda i: (jnp.maximum(i * per - 1, 0), 0)),
            pl.BlockSpec((POOL_HALO, GROUP_WIDTH), lambda i: (jnp.minimum(i * per + per, last), 0)),
            pl.BlockSpec((1, GROUP_WIDTH, GROUP_WIDTH), lambda i: (layer, 0, 0)),
            pl.BlockSpec((1, 1, GROUP_WIDTH), lambda i: (layer, 0, 0)),
        ],
        out_specs=pl.BlockSpec((r, GROUP_WIDTH), lambda i: (i, 0)),
        out_shape=jax.ShapeDtypeStruct((t, GROUP_WIDTH), BF16),
        compiler_params=_cparams(("arbitrary",)),
        name="pool",
    )(u, u, u, p["pool_bd"], p["pool_scale"])


def _mix_ffn_kernel(x_ref, ota_ref, otc_ref, otd_ref, yb_ref, wout_ref, mod_ref, gpost1_ref,
                    gpre2_ref, wgu_ref, wdown_ref, gpost2_ref, o_ref):
    subs = [slice(k, k + SUB_TILE) for k in range(0, x_ref.shape[0], SUB_TILE)]
    n_att = ota_ref.shape[0] + otc_ref.shape[0] + otd_ref.shape[0]
    ys = []
    for sl in subs:
        ot = jnp.concatenate([ota_ref[:, sl], otc_ref[:, sl], otd_ref[:, sl]], axis=0)
        y = lax.dot_general(ot, wout_ref[0, 0:n_att, :], (((0,), (0,)), ((), ())), preferred_element_type=F32)
        ys.append(y + jnp.dot(yb_ref[sl, :], wout_ref[0, n_att:, :], preferred_element_type=F32))
    xs = [x_ref[sl, :] + mod_ref[0, 0, 2:3, :] * _rms(y, gpost1_ref[0]) for sl, y in zip(subs, ys)]
    gus = []
    for x in xs:
        h = _rms(x, gpre2_ref[0])
        h = h * (1.0 + mod_ref[0, 0, 4:5, :]) + mod_ref[0, 0, 3:4, :]
        gus.append(jnp.dot(h.astype(BF16), wgu_ref[0], preferred_element_type=F32))
    y2s = []
    for gu in gus:
        gate, up = gu[:, :D_FF], gu[:, D_FF:]
        act = (gate * jax.nn.sigmoid(gate) * up).astype(BF16)
        y2s.append(jnp.dot(act, wdown_ref[0], preferred_element_type=F32))
    for sl, x, y2 in zip(subs, xs, y2s):
        o_ref[sl, :] = x + mod_ref[0, 0, 5:6, :] * _rms(y2, gpost2_ref[0])


def _mix_ffn(x, ot_a, ot_c, ot_d, yb, mod, stream, layer, p):
    t = x.shape[0]
    r = min(ROW_TILE, t)
    feat = pl.BlockSpec((GROUP_WIDTH, r), lambda i: (0, i))
    lay3 = lambda i: (layer, 0, 0)
    once = pl.Buffered(1)
    return pl.pallas_call(
        _mix_ffn_kernel,
        grid=(t // r,),
        in_specs=[
            pl.BlockSpec((r, D_MODEL), lambda i: (i, 0)),
            feat, feat, feat,
            pl.BlockSpec((r, GROUP_WIDTH), lambda i: (i, 0)),
            pl.BlockSpec((1, D_MODEL, D_MODEL), lay3, pipeline_mode=once),
            pl.BlockSpec((1, 1, N_MOD, D_MODEL), lambda i: (layer, stream, 0, 0)),
            pl.BlockSpec((1, 1, D_MODEL), lay3),
            pl.BlockSpec((1, 1, D_MODEL), lay3),
            pl.BlockSpec((1, D_MODEL, 2 * D_FF), lay3, pipeline_mode=once),
            pl.BlockSpec((1, D_FF, D_MODEL), lay3, pipeline_mode=once),
            pl.BlockSpec((1, 1, D_MODEL), lay3),
        ],
        out_specs=pl.BlockSpec((r, D_MODEL), lambda i: (i, 0)),
        out_shape=jax.ShapeDtypeStruct((t, D_MODEL), F32),
        compiler_params=_cparams(("arbitrary",)),
        name="mix_ffn",
    )(x, ot_a, ot_c, ot_d, yb, p["w_out"], mod, p["g_post1"], p["g_pre2"], p["w_gu"], p["w_down"], p["g_post2"])


def _rope_tables(n):
    rows = n // GRID_W
    row = jnp.repeat(jnp.arange(rows, dtype=F32), GRID_W)
    col = jnp.tile(jnp.arange(GRID_W, dtype=F32), rows)
    out = []
    for rot_dim in (HEAD_DIM, C_ROPE):
        n_axis = rot_dim // 4
        freqs = ROPE_THETA ** (-jnp.arange(n_axis, dtype=F32) / n_axis)
        ang = jnp.concatenate([row[:, None] * freqs, col[:, None] * freqs], axis=-1)
        out += [jnp.cos(ang).T, jnp.sin(ang).T]
    return tuple(out)


def _identity_rope(n):
    return (jnp.ones((HEAD_DIM // 2, n), F32), jnp.zeros((HEAD_DIM // 2, n), F32),
            jnp.ones((C_ROPE // 2, n), F32), jnp.zeros((C_ROPE // 2, n), F32))


def _pack_params(w_in, pool_w, w_out, w_gu, w_down, c_wuq, c_wukv):
    depth = w_in.shape[0]
    pad = jnp.zeros((depth, D_MODEL, IN_PACKED - IN_WIDTH), w_in.dtype)
    w_in_p = jnp.concatenate([w_in[:, :, :KR_END], pad, w_in[:, :, KR_END:]], axis=2).astype(BF16)
    eye = jnp.eye(len(POOL_SIZES), dtype=pool_w.dtype)
    pool_bd = jnp.einsum("lgce,gh->lgche", pool_w, eye).reshape(depth, GROUP_WIDTH, GROUP_WIDTH).astype(BF16)
    w_out_p = jnp.concatenate([w_out[:, 0:256], w_out[:, 512:1024], w_out[:, 256:512]], axis=1).astype(BF16)
    return dict(w_in=w_in_p, pool_bd=pool_bd, w_out=w_out_p, w_gu=w_gu.astype(BF16), w_down=w_down.astype(BF16),
                c_wuq=c_wuq.astype(BF16), c_wukv=c_wukv.astype(BF16))


def kernel(x, c, ctx, c_ctx, w_ada, b_ada, g_pre1, g_post1, w_in, a_sink, pool_w, pool_scale, c_gq, c_wuq, c_gkv,
           c_wukv, d_gq, d_gk, w_out, g_pre2, g_post2, w_gu, w_down):
    assert x.shape[0] == 1 and ctx.shape[0] == 1
    depth = w_in.shape[0]
    n = x.shape[1]
    n_ctx = ctx.shape[1]
    p = _pack_params(w_in, pool_w, w_out, w_gu, w_down, c_wuq, c_wukv)
    row3 = lambda a: a.reshape(depth, 1, a.shape[1])
    p.update(g_pre1=row3(g_pre1), g_post1=row3(g_post1), g_pre2=row3(g_pre2), g_post2=row3(g_post2),
             pool_scale=row3(pool_scale), c_gq=row3(c_gq), c_gkv=row3(c_gkv),
             d_gq=d_gq.reshape(depth, HEAD_DIM, 1), d_gk=d_gk.reshape(depth, HEAD_DIM, 1))

    cond = jnp.zeros((8, D_MODEL), F32).at[0].set(c[0]).at[1].set(c_ctx)
    mod = _modulation(cond, w_ada, b_ada).reshape(depth, 8, N_MOD, D_MODEL)

    rope_x = _rope_tables(n)
    rope_c = _identity_rope(n_ctx)
    xs, cs = x[0], ctx[0]
    for layer in range(depth):
        last = layer == depth - 1
        sink = a_sink[layer] * LOG2E
        qta, ka, vta, ub, qtc, kc, vtc, qtd, kd, vtd = _project(xs, mod, 0, layer, p, rope_x)
        qta_c, ka_c, vta_c, ub_c, qtc_c, kc_c, vtc_c, qtd_c, kd_c, vtd_c = _project(cs, mod, 1, layer, p, rope_c)

        ot_a = _window_attention(qta, ka_c, vta_c, ka, vta, sink)
        yb = _pool(ub, layer, p)
        ot_c = _dense_attention(qtc, kc_c, vtc_c, kc, vtc, name="dense_attn_c")
        ot_d = _dense_attention(qtd, kd_c, vtd_c, kd, vtd, name="dense_attn_d")
        xs = _mix_ffn(xs, ot_a, ot_c, ot_d, yb, mod, 0, layer, p)

        if not last:
            ot_a_c = _dense_attention(qta_c, ka_c, vta_c, sink=sink, name="ctx_attn_a")
            yb_c = _pool(ub_c, layer, p)
            ot_c_c = _dense_attention(qtc_c, kc_c, vtc_c, name="ctx_attn_c")
            ot_d_c = _dense_attention(qtd_c, kd_c, vtd_c, name="ctx_attn_d")
            cs = _mix_ffn(cs, ot_a_c, ot_c_c, ot_d_c, yb_c, mod, 1, layer, p)
    return xs[None]
```

```python
import functools

import numpy as np
import jax
import jax.numpy as jnp
from jax import lax
from jax.experimental import pallas as pl
from jax.experimental.pallas import tpu as pltpu

F32 = jnp.float32
BF16 = jnp.bfloat16

D_MODEL = 1024
GRID_W = 64
GROUP_WIDTH = 256
HEAD_DIM = 64
ROPE_THETA = 10000.0
EPS = 1e-6
NEG_INF = -1e30
LOG2E = 1.4426950408889634
N_MOD = 6
N_HEADS = 4
KV_HEADS = 2
WINDOW = 128
POOL_SIZES = (2, 4, 8, 16)
POOL_GROUP = 64
C_NOPE, C_ROPE, C_V = 64, 32, 64
C_Q_RANK, C_KV_RANK = 256, 128
D_FF = 2816
IN_WIDTH = 1696
KR_END = 1184
IN_PACKED = 1792

LANE = 128
QK_PAD = 128
V_ROWS = 80
ROW_TILE = 512
SUB_TILE = 256
Q_TILE = 512
FLASH_Q_TILE = 512
KV_BLOCK = 512
MXU_DEPTH = 256
MAX_TRIP_BLOCKS = 6
HALO = 128
POOL_HALO = 8
VMEM_LIMIT = 56 * 1024 * 1024


def _cparams(sem):
    return pltpu.CompilerParams(dimension_semantics=sem, vmem_limit_bytes=VMEM_LIMIT)


def _rms(x, g):
    ms = jnp.mean(x * x, axis=-1, keepdims=True)
    return x * lax.rsqrt(ms + EPS) * g


def _split_bf16(x):
    hi = x.astype(BF16)
    lo = (x - hi.astype(F32)).astype(BF16)
    return hi, lo


def _mod_kernel(cond_ref, w_ref, b_ref, o_ref):
    c = cond_ref[...]
    s = c * jax.nn.sigmoid(c)
    s_hi, s_lo = _split_bf16(s)
    w_hi, w_lo = _split_bf16(w_ref[0])
    acc = jnp.dot(s_hi, w_hi, preferred_element_type=F32)
    acc += jnp.dot(s_hi, w_lo, preferred_element_type=F32)
    acc += jnp.dot(s_lo, w_hi, preferred_element_type=F32)
    o_ref[0] = acc + b_ref[0]


def _modulation(cond, w_ada, b_ada):
    depth, d, n = w_ada.shape
    tn = 1536
    return pl.pallas_call(
        _mod_kernel,
        grid=(depth, n // tn),
        in_specs=[
            pl.BlockSpec((8, d), lambda l, j: (0, 0)),
            pl.BlockSpec((1, d, tn), lambda l, j: (l, 0, j)),
            pl.BlockSpec((1, 1, tn), lambda l, j: (l, 0, j)),
        ],
        out_specs=pl.BlockSpec((1, 8, tn), lambda l, j: (l, 0, j)),
        out_shape=jax.ShapeDtypeStruct((depth, 8, n), F32),
        compiler_params=_cparams(("arbitrary", "arbitrary")),
        name="modulation",
    )(cond, w_ada, b_ada.reshape(depth, 1, n))


def _rope_t(xt, cos, sin):
    half = xt.shape[0] // 2
    x1, x2 = xt[:half], xt[half:]
    return jnp.concatenate([x1 * cos - x2 * sin, x2 * cos + x1 * sin], axis=0)


def _ones_rows(r):
    row = lax.broadcasted_iota(jnp.int32, (V_ROWS - HEAD_DIM, r), 0)
    return jnp.where(row == 0, 1.0, 0.0).astype(BF16)


def _proj_kernel(x_ref, mod_ref, gpre_ref, win_ref, cgq_ref, cwuq_ref, cgkv_ref, cwukv_ref,
                 dgq_ref, dgk_ref, cos_hd_ref, sin_hd_ref, cos_c_ref, sin_c_ref,
                 qta_ref, ka_ref, vta_ref, u_ref, qtc_ref, kc_ref, vtc_ref, qtd_ref, kd_ref, vtd_ref):
    subs = [slice(k, k + SUB_TILE) for k in range(0, x_ref.shape[0], SUB_TILE)]
    zs = []
    for sl in subs:
        h = _rms(x_ref[sl, :], gpre_ref[0])
        h = h * (1.0 + mod_ref[0, 0, 1:2, :]) + mod_ref[0, 0, 0:1, :]
        zs.append(jnp.dot(h.astype(BF16), win_ref[0], preferred_element_type=F32))
    for sl, z in zip(subs, zs):
        _proj_groups(z, sl, cgq_ref, cwuq_ref, cgkv_ref, cwukv_ref, dgq_ref, dgk_ref,
                     cos_hd_ref, sin_hd_ref, cos_c_ref, sin_c_ref,
                     qta_ref, ka_ref, vta_ref, u_ref, qtc_ref, kc_ref, vtc_ref, qtd_ref, kd_ref, vtd_ref)


def _proj_groups(z, sl, cgq_ref, cwuq_ref, cgkv_ref, cwukv_ref, dgq_ref, dgk_ref,
                 cos_hd_ref, sin_hd_ref, cos_c_ref, sin_c_ref,
                 qta_ref, ka_ref, vta_ref, u_ref, qtc_ref, kc_ref, vtc_ref, qtd_ref, kd_ref, vtd_ref):
    r = z.shape[0]
    scale_hd = HEAD_DIM ** -0.5 * LOG2E
    scale_c = (C_NOPE + C_ROPE) ** -0.5 * LOG2E
    cos_hd, sin_hd = cos_hd_ref[:, sl], sin_hd_ref[:, sl]
    cos_c, sin_c = cos_c_ref[:, sl], sin_c_ref[:, sl]
    ones = _ones_rows(r)
    zeros64 = jnp.zeros((HEAD_DIM, r), BF16)

    def put_gqa_q(ref, qt, hd):
        g = hd // (N_HEADS // KV_HEADS)
        ref[hd, g * HEAD_DIM:(g + 1) * HEAD_DIM, sl] = (qt * scale_hd).astype(BF16)
        ref[hd, (1 - g) * HEAD_DIM:(2 - g) * HEAD_DIM, sl] = zeros64

    def put_v(ref, hd, vt):
        ref[hd, 0:HEAD_DIM, sl] = vt.astype(BF16)
        ref[hd, HEAD_DIM:V_ROWS, sl] = ones

    qa_t = z[:, 0:256].T
    for hd in range(N_HEADS):
        put_gqa_q(qta_ref, _rope_t(qa_t[hd * 64:(hd + 1) * 64], cos_hd, sin_hd), hd)
    ka_t = z[:, 256:384].T
    ka_t = jnp.concatenate([_rope_t(ka_t[g * 64:(g + 1) * 64], cos_hd, sin_hd) for g in range(KV_HEADS)], axis=0)
    ka_ref[0, sl, :] = ka_t.T.astype(BF16)
    va_t = z[:, 384:512].T
    for g in range(KV_HEADS):
        put_v(vta_ref, g, va_t[g * 64:(g + 1) * 64])

    u_ref[sl, :] = z[:, 512:768]

    cq = _rms(z[:, 768:1024], cgq_ref[0])
    qc_t = jnp.dot(cq.astype(BF16), cwuq_ref[0], preferred_element_type=F32).T
    ckv = _rms(z[:, 1024:1152], cgkv_ref[0])
    kvc_t = jnp.dot(ckv.astype(BF16), cwukv_ref[0], preferred_element_type=F32).T
    kr_t = _rope_t(z[:, 1152:1280].T[0:C_ROPE], cos_c, sin_c)
    pad32 = jnp.zeros((QK_PAD - C_NOPE - C_ROPE, r), F32)
    per_head_q = C_NOPE + C_ROPE
    for hd in range(N_HEADS):
        q_nope = qc_t[hd * per_head_q:hd * per_head_q + C_NOPE]
        q_rope = _rope_t(qc_t[hd * per_head_q + C_NOPE:(hd + 1) * per_head_q], cos_c, sin_c)
        qtc_ref[hd, :, sl] = (jnp.concatenate([q_nope, q_rope, pad32], axis=0) * scale_c).astype(BF16)
        k_nope = kvc_t[hd * 128:hd * 128 + C_NOPE]
        kc_ref[hd, sl, :] = jnp.concatenate([k_nope, kr_t, pad32], axis=0).T.astype(BF16)
        put_v(vtc_ref, hd, kvc_t[hd * 128 + C_NOPE:(hd + 1) * 128])

    def head_norm(xt, g_col):
        ms = jnp.mean(xt * xt, axis=0, keepdims=True)
        return xt * lax.rsqrt(ms + EPS) * g_col

    qd_t = z[:, 1280:1536].T
    for hd in range(N_HEADS):
        q = head_norm(qd_t[hd * 64:(hd + 1) * 64], dgq_ref[0])
        put_gqa_q(qtd_ref, _rope_t(q, cos_hd, sin_hd), hd)
    kd_t = z[:, 1536:1664].T
    kd_t = jnp.concatenate(
        [_rope_t(head_norm(kd_t[g * 64:(g + 1) * 64], dgk_ref[0]), cos_hd, sin_hd) for g in range(KV_HEADS)], axis=0)
    kd_ref[0, sl, :] = kd_t.T.astype(BF16)
    vd_t = z[:, 1664:1792].T
    for g in range(KV_HEADS):
        put_v(vtd_ref, g, vd_t[g * 64:(g + 1) * 64])


def _project(x, mod, stream, layer, p, rope):
    t = x.shape[0]
    r = min(ROW_TILE, t)
    cos_hd, sin_hd, cos_c, sin_c = rope
    lay3 = lambda i: (layer, 0, 0)
    tok = lambda i: (i, 0)
    feat3 = lambda i: (0, 0, i)
    tok3 = lambda i: (0, i, 0)
    out_shapes = (
        jax.ShapeDtypeStruct((N_HEADS, QK_PAD, t), BF16),
        jax.ShapeDtypeStruct((1, t, QK_PAD), BF16),
        jax.ShapeDtypeStruct((KV_HEADS, V_ROWS, t), BF16),
        jax.ShapeDtypeStruct((t, GROUP_WIDTH), F32),
        jax.ShapeDtypeStruct((N_HEADS, QK_PAD, t), BF16),
        jax.ShapeDtypeStruct((N_HEADS, t, QK_PAD), BF16),
        jax.ShapeDtypeStruct((N_HEADS, V_ROWS, t), BF16),
        jax.ShapeDtypeStruct((N_HEADS, QK_PAD, t), BF16),
        jax.ShapeDtypeStruct((1, t, QK_PAD), BF16),
        jax.ShapeDtypeStruct((KV_HEADS, V_ROWS, t), BF16),
    )
    out_specs = (
        pl.BlockSpec((N_HEADS, QK_PAD, r), feat3),
        pl.BlockSpec((1, r, QK_PAD), tok3),
        pl.BlockSpec((KV_HEADS, V_ROWS, r), feat3),
        pl.BlockSpec((r, GROUP_WIDTH), tok),
        pl.BlockSpec((N_HEADS, QK_PAD, r), feat3),
        pl.BlockSpec((N_HEADS, r, QK_PAD), tok3),
        pl.BlockSpec((N_HEADS, V_ROWS, r), feat3),
        pl.BlockSpec((N_HEADS, QK_PAD, r), feat3),
        pl.BlockSpec((1, r, QK_PAD), tok3),
        pl.BlockSpec((KV_HEADS, V_ROWS, r), feat3),
    )
    in_specs = [
        pl.BlockSpec((r, D_MODEL), tok),
        pl.BlockSpec((1, 1, N_MOD, D_MODEL), lambda i: (layer, stream, 0, 0)),
        pl.BlockSpec((1, 1, D_MODEL), lay3),
        pl.BlockSpec((1, D_MODEL, IN_PACKED), lay3),
        pl.BlockSpec((1, 1, C_Q_RANK), lay3),
        pl.BlockSpec((1, C_Q_RANK, p["c_wuq"].shape[2]), lay3),
        pl.BlockSpec((1, 1, C_KV_RANK), lay3),
        pl.BlockSpec((1, C_KV_RANK, p["c_wukv"].shape[2]), lay3),
        pl.BlockSpec((1, HEAD_DIM, 1), lay3),
        pl.BlockSpec((1, HEAD_DIM, 1), lay3),
        pl.BlockSpec((HEAD_DIM // 2, r), lambda i: (0, i)),
        pl.BlockSpec((HEAD_DIM // 2, r), lambda i: (0, i)),
        pl.BlockSpec((C_ROPE // 2, r), lambda i: (0, i)),
        pl.BlockSpec((C_ROPE // 2, r), lambda i: (0, i)),
    ]
    return pl.pallas_call(
        _proj_kernel,
        grid=(t // r,),
        in_specs=in_specs,
        out_specs=out_specs,
        out_shape=out_shapes,
        compiler_params=_cparams(("arbitrary",)),
        name="in_proj",
    )(x, mod, p["g_pre1"], p["w_in"], p["c_gq"], p["c_wuq"], p["c_gkv"], p["c_wukv"],
      p["d_gq"], p["d_gk"], cos_hd, sin_hd, cos_c, sin_c)


def _trip_blocks(n_blocks):
    return max(b for b in range(2, MAX_TRIP_BLOCKS + 1, 2) if (n_blocks - 2) % b == 0)


def _flash_kernel(*refs, n_blocks, has_sink, kv_block, rotate, s_pad):
    refs = list(refs)
    sink_ref = refs.pop(0) if has_sink else None
    qt_ref, kc_ref, vtc_ref = refs[:3]
    refs = refs[3:]
    if n_blocks:
        kl_ref, vtl_ref = refs[:2]
        refs = refs[2:]
    o_ref, m_ref, acc_ref = refs[:3]
    scratch = refs[3:]

    q = qt_ref[0]
    tq = q.shape[1]
    s = jnp.dot(kc_ref[0], q, preferred_element_type=F32)

    if n_blocks:
        s_refs, cm_refs, p_refs, al_refs = scratch[0:2], scratch[2:4], scratch[4:6], scratch[6:8]
        halves = kv_block // MXU_DEPTH

        def produce_part(slot, j, h, cm):
            off = pl.multiple_of(j * kv_block + h * MXU_DEPTH, MXU_DEPTH)
            sj = jnp.dot(kl_ref[0, pl.ds(off, MXU_DEPTH), :], q, preferred_element_type=F32)
            s_refs[slot][h * MXU_DEPTH:(h + 1) * MXU_DEPTH, 0:tq] = sj
            cmh = jnp.max(sj, axis=0, keepdims=True)
            return cmh if cm is None else jnp.maximum(cm, cmh)

        def soften(slot):
            m_old = m_ref[...]
            m_new = jnp.maximum(m_old, cm_refs[slot][...])
            al_refs[slot][...] = jnp.exp2(m_old - m_new)
            p_refs[slot][...] = jnp.exp2(s_refs[slot][:, 0:tq] - m_new).astype(BF16)
            m_ref[...] = m_new

        def accumulate_part(slot, j, h):
            off = pl.multiple_of(j * kv_block + h * MXU_DEPTH, MXU_DEPTH)
            pv = jnp.dot(vtl_ref[0, :, pl.ds(off, MXU_DEPTH)], p_refs[slot][h * MXU_DEPTH:(h + 1) * MXU_DEPTH, :],
                         preferred_element_type=F32)
            scale = al_refs[slot][...] if h == 0 else 1.0
            acc_ref[...] = scale * acc_ref[...] + pv

        def stages(slot_p, j_p, slot_a, j_a):
            cm = None
            for h in range(halves):
                if j_p is not None:
                    cm = produce_part(slot_p, j_p, h, cm)
                if j_a is not None:
                    accumulate_part(slot_a, j_a, h)
            if j_p is not None:
                cm_refs[slot_p][...] = cm

        stages(0, 0, None, None)

    m = jnp.max(s, axis=0, keepdims=True)
    if has_sink:
        sink = sink_ref[pl.program_id(0)]
        m = jnp.maximum(m, sink)
    p = jnp.exp2(s - m)
    acc = jnp.dot(vtc_ref[0], p.astype(BF16), preferred_element_type=F32)
    if has_sink:
        row = lax.broadcasted_iota(jnp.int32, acc.shape, 0)
        acc = acc + jnp.where(row == HEAD_DIM, jnp.exp2(sink - m), 0.0)

    if n_blocks:
        m_ref[...] = m
        acc_ref[...] = acc
        p_refs[1][...] = jnp.zeros(p_refs[1].shape, BF16)
        al_refs[1][...] = jnp.ones(al_refs[1].shape, F32)
        per_trip = _trip_blocks(n_blocks)
        stages(1, 1, None, None)
        if rotate:
            soften(0)

            def trip(t, carry):
                for b in range(per_trip):
                    j = t * per_trip + b
                    stages(b % 2, j + 2, 1 - b % 2, jnp.maximum(j - 1, 0))
                    soften(1 - b % 2)
                return carry

            lax.fori_loop(0, (n_blocks - 2) // per_trip, trip, 0)
            stages(None, None, 1, n_blocks - 3)
            soften(1)
            stages(None, None, 0, n_blocks - 2)
            stages(None, None, 1, n_blocks - 1)
        else:
            def trip(t, carry):
                for b in range(per_trip):
                    j = t * per_trip + b
                    soften(b % 2)
                    stages(b % 2, j + 2, 1 - b % 2, jnp.maximum(j - 1, 0))
                return carry

            lax.fori_loop(0, (n_blocks - 2) // per_trip, trip, 0)
            for j in (n_blocks - 2, n_blocks - 1):
                soften(j % 2)
                stages(None, None, 1 - j % 2, j - 1)
            stages(None, None, (n_blocks - 1) % 2, n_blocks - 1)
        acc = acc_ref[...]
    o_ref[...] = (acc[0:HEAD_DIM] / acc[HEAD_DIM:HEAD_DIM + 1]).astype(o_ref.dtype)


def _dense_attention(qt, k_ctx, vt_ctx, k_lat=None, vt_lat=None, sink=None, name="dense_attn",
                     q_tile=FLASH_Q_TILE, kv_block=KV_BLOCK, rotate=True, s_pad=0):
    n_heads, _, tq_total = qt.shape
    tq = min(q_tile, tq_total)
    n_ctx = k_ctx.shape[1]
    k_rep = n_heads // k_ctx.shape[0]
    v_rep = n_heads // vt_ctx.shape[0]
    has_sink = sink is not None
    has_lat = k_lat is not None
    n_lat = k_lat.shape[1] if has_lat else 0
    n_blocks = n_lat // kv_block
    assert n_lat % (2 * kv_block) == 0 and n_blocks != 2
    args, in_specs = [], []
    if has_sink:
        args.append(sink)
        in_specs.append(pl.BlockSpec(memory_space=pltpu.SMEM))
    args += [qt, k_ctx, vt_ctx]
    in_specs += [
        pl.BlockSpec((1, QK_PAD, tq), lambda h, i: (h, 0, i)),
        pl.BlockSpec((1, n_ctx, QK_PAD), lambda h, i: (h // k_rep, 0, 0)),
        pl.BlockSpec((1, V_ROWS, n_ctx), lambda h, i: (h // v_rep, 0, 0)),
    ]
    scratch = [pltpu.VMEM((1, tq), F32), pltpu.VMEM((V_ROWS, tq), F32)]
    if has_lat:
        args += [k_lat, vt_lat]
        in_specs += [
            pl.BlockSpec((1, n_lat, QK_PAD), lambda h, i: (h // k_rep, 0, 0)),
            pl.BlockSpec((1, V_ROWS, n_lat), lambda h, i: (h // v_rep, 0, 0)),
        ]
        scratch += (2 * [pltpu.VMEM((kv_block, tq + s_pad), F32)] + 2 * [pltpu.VMEM((1, tq), F32)]
                    + 2 * [pltpu.VMEM((kv_block, tq), BF16)] + 2 * [pltpu.VMEM((1, tq), F32)])
    return pl.pallas_call(
        functools.partial(_flash_kernel, n_blocks=n_blocks, has_sink=has_sink, kv_block=kv_block, rotate=rotate,
                          s_pad=s_pad),
        grid=(n_heads, tq_total // tq),
        in_specs=in_specs,
        out_specs=pl.BlockSpec((HEAD_DIM, tq), lambda h, i: (h, i)),
        out_shape=jax.ShapeDtypeStruct((n_heads * HEAD_DIM, tq_total), BF16),
        scratch_shapes=scratch,
        compiler_params=_cparams(("arbitrary", "arbitrary")),
        name=name,
    )(*args)


def _window_kernel(sink_ref, qt_ref, kc_ref, vtc_ref, kp_ref, km_ref, kn_ref, vp_ref, vm_ref, vn_ref,
                   bias_ref, o_ref):
    i = pl.program_id(0)
    n = pl.num_programs(0)
    n_heads = qt_ref.shape[0]
    v_rep = n_heads // vtc_ref.shape[0]
    dot = functools.partial(jnp.dot, preferred_element_type=F32)
    edge_prev = jnp.where(i == 0, NEG_INF, 0.0).astype(F32)
    edge_next = jnp.where(i == n - 1, NEG_INF, 0.0).astype(F32)
    bias_p = bias_ref[0:HALO, :] + edge_prev
    bias_m = bias_ref[HALO:HALO + Q_TILE, :]
    bias_n = bias_ref[HALO + Q_TILE:, :] + edge_next

    def scores(hd):
        q = qt_ref[hd]
        return (dot(kc_ref[0], q), dot(kp_ref[0], q) + bias_p, dot(km_ref[0], q) + bias_m,
                dot(kn_ref[0], q) + bias_n)

    def finish(hd, ss):
        g = hd // v_rep
        sink = sink_ref[hd]
        m = functools.reduce(jnp.maximum, [jnp.max(s, axis=0, keepdims=True) for s in ss])
        m = jnp.maximum(m, sink)
        v_refs = (vtc_ref, vp_ref, vm_ref, vn_ref)
        acc = sum(dot(v_ref[g], jnp.exp2(s - m).astype(BF16)) for v_ref, s in zip(v_refs, ss))
        denom = acc[HEAD_DIM:HEAD_DIM + 1] + jnp.exp2(sink - m)
        o_ref[hd * HEAD_DIM:(hd + 1) * HEAD_DIM, :] = (acc[0:HEAD_DIM] / denom).astype(o_ref.dtype)

    ss = scores(0)
    for hd in range(n_heads):
        nxt = scores(hd + 1) if hd + 1 < n_heads else None
        finish(hd, ss)
        ss = nxt


def _window_bias():
    r = np.arange(Q_TILE + 2 * HALO)[:, None] - HALO
    c = np.arange(Q_TILE)[None, :]
    return np.where(np.abs(c - r) <= WINDOW, 0.0, NEG_INF).astype(np.float32)


def _window_attention(qt, k_ctx, vt_ctx, k_lat, vt_lat, sink):
    n_heads, _, t = qt.shape
    n_ctx = k_ctx.shape[1]
    n_kv = vt_ctx.shape[0]
    per = Q_TILE // HALO
    last = t // HALO - 1
    prev_blk = lambda i: jnp.maximum(i * per - 1, 0)
    next_blk = lambda i: jnp.minimum(i * per + per, last)
    in_specs = [
        pl.BlockSpec(memory_space=pltpu.SMEM),
        pl.BlockSpec((n_heads, QK_PAD, Q_TILE), lambda i: (0, 0, i)),
        pl.BlockSpec((1, n_ctx, QK_PAD), lambda i: (0, 0, 0)),
        pl.BlockSpec((n_kv, V_ROWS, n_ctx), lambda i: (0, 0, 0)),
        pl.BlockSpec((1, HALO, QK_PAD), lambda i: (0, prev_blk(i), 0)),
        pl.BlockSpec((1, Q_TILE, QK_PAD), lambda i: (0, i, 0)),
        pl.BlockSpec((1, HALO, QK_PAD), lambda i: (0, next_blk(i), 0)),
        pl.BlockSpec((n_kv, V_ROWS, HALO), lambda i: (0, 0, prev_blk(i))),
        pl.BlockSpec((n_kv, V_ROWS, Q_TILE), lambda i: (0, 0, i)),
        pl.BlockSpec((n_kv, V_ROWS, HALO), lambda i: (0, 0, next_blk(i))),
        pl.BlockSpec((Q_TILE + 2 * HALO, Q_TILE), lambda i: (0, 0)),
    ]
    return pl.pallas_call(
        _window_kernel,
        grid=(t // Q_TILE,),
        in_specs=in_specs,
        out_specs=pl.BlockSpec((n_heads * HEAD_DIM, Q_TILE), lambda i: (0, i)),
        out_shape=jax.ShapeDtypeStruct((n_heads * HEAD_DIM, t), BF16),
        compiler_params=_cparams(("arbitrary",)),
        name="window_attn",
    )(sink, qt, k_ctx, vt_ctx, k_lat, k_lat, k_lat, vt_lat, vt_lat, vt_lat, jnp.asarray(_window_bias()))


def _pool_kernel(u_ref, up_ref, un_ref, w_ref, scale_ref, o_ref, *, t_total):
    i = pl.program_id(0)
    n = pl.num_programs(0)
    r = u_ref.shape[0]
    u = u_ref[...]
    prev = jnp.where(i > 0, up_ref[...], 0.0)
    nxt = jnp.where(i < n - 1, un_ref[...], 0.0)
    ext = jnp.concatenate([prev, u, nxt], axis=0)
    rows = ext.shape[0]

    def shifted(x, k):
        return pltpu.roll(x, k % rows, axis=0)

    def window_sums(e):
        s2 = shifted(e, 1) + e
        s4 = shifted(s2, 1) + shifted(s2, -1)
        s8 = shifted(s4, 2) + shifted(s4, -2)
        s16 = shifted(s8, 4) + shifted(s8, -4)
        return s2, s4, s8, s16

    low_group = lax.broadcasted_iota(jnp.int32, (rows, LANE), 1) < POOL_GROUP
    s2, s4, _, _ = window_sums(ext[:, 0:LANE])
    _, _, s8, s16 = window_sums(ext[:, LANE:2 * LANE])
    total = jnp.concatenate([jnp.where(low_group, s2, s4), jnp.where(low_group, s8, s16)], axis=1)
    total = total[POOL_HALO:POOL_HALO + r]

    lane = lax.broadcasted_iota(jnp.int32, (r, GROUP_WIDTH), 1)
    pos = lax.broadcasted_iota(jnp.int32, (r, GROUP_WIDTH), 0) + i * r
    half = jnp.zeros((r, GROUP_WIDTH), jnp.int32)
    for gi, size in enumerate(POOL_SIZES):
        half = jnp.where((lane >= gi * POOL_GROUP) & (lane < (gi + 1) * POOL_GROUP), size // 2, half)
    count = jnp.minimum(pos + half, t_total) - jnp.maximum(pos - half, 0)
    y = total / count.astype(F32) - u
    o_ref[...] = (jnp.dot(y.astype(BF16), w_ref[0], preferred_element_type=F32) * scale_ref[0]).astype(o_ref.dtype)


def _pool(u, layer, p):
    t = u.shape[0]
    r = min(ROW_TILE, t)
    per = r // POOL_HALO
    last = t // POOL_HALO - 1
    return pl.pallas_call(
        functools.partial(_pool_kernel, t_total=t),
        grid=(t // r,),
        in_specs=[
            pl.BlockSpec((r, GROUP_WIDTH), lambda i: (i, 0)),
            pl.BlockSpec((POOL_HALO, GROUP_WIDTH), lambda i: (jnp.maximum(i * per - 1, 0), 0)),
            pl.BlockSpec((POOL_HALO, GROUP_WIDTH), lambda i: (jnp.minimum(i * per + per, last), 0)),
            pl.BlockSpec((1, GROUP_WIDTH, GROUP_WIDTH), lambda i: (layer, 0, 0)),
            pl.BlockSpec((1, 1, GROUP_WIDTH), lambda i: (layer, 0, 0)),
        ],
        out_specs=pl.BlockSpec((r, GROUP_WIDTH), lambda i: (i, 0)),
        out_shape=jax.ShapeDtypeStruct((t, GROUP_WIDTH), BF16),
        compiler_params=_cparams(("arbitrary",)),
        name="pool",
    )(u, u, u, p["pool_bd"], p["pool_scale"])


def _mix_ffn_kernel(x_ref, ota_ref, otc_ref, otd_ref, yb_ref, wout_ref, mod_ref, gpost1_ref,
                    gpre2_ref, wgu_ref, wdown_ref, gpost2_ref, o_ref):
    subs = [slice(k, k + SUB_TILE) for k in range(0, x_ref.shape[0], SUB_TILE)]
    n_att = ota_ref.shape[0] + otc_ref.shape[0] + otd_ref.shape[0]
    ys = []
    for sl in subs:
        ot = jnp.concatenate([ota_ref[:, sl], otc_ref[:, sl], otd_ref[:, sl]], axis=0)
        y = lax.dot_general(ot, wout_ref[0, 0:n_att, :], (((0,), (0,)), ((), ())), preferred_element_type=F32)
        ys.append(y + jnp.dot(yb_ref[sl, :], wout_ref[0, n_att:, :], preferred_element_type=F32))
    xs = [x_ref[sl, :] + mod_ref[0, 0, 2:3, :] * _rms(y, gpost1_ref[0]) for sl, y in zip(subs, ys)]
    gus = []
    for x in xs:
        h = _rms(x, gpre2_ref[0])
        h = h * (1.0 + mod_ref[0, 0, 4:5, :]) + mod_ref[0, 0, 3:4, :]
        gus.append(jnp.dot(h.astype(BF16), wgu_ref[0], preferred_element_type=F32))
    y2s = []
    for gu in gus:
        gate, up = gu[:, :D_FF], gu[:, D_FF:]
        act = (gate * jax.nn.sigmoid(gate) * up).astype(BF16)
        y2s.append(jnp.dot(act, wdown_ref[0], preferred_element_type=F32))
    for sl, x, y2 in zip(subs, xs, y2s):
        o_ref[sl, :] = x + mod_ref[0, 0, 5:6, :] * _rms(y2, gpost2_ref[0])


def _mix_ffn(x, ot_a, ot_c, ot_d, yb, mod, stream, layer, p):
    t = x.shape[0]
    r = min(ROW_TILE, t)
    feat = pl.BlockSpec((GROUP_WIDTH, r), lambda i: (0, i))
    lay3 = lambda i: (layer, 0, 0)
    once = pl.Buffered(1)
    return pl.pallas_call(
        _mix_ffn_kernel,
        grid=(t // r,),
        in_specs=[
            pl.BlockSpec((r, D_MODEL), lambda i: (i, 0)),
            feat, feat, feat,
            pl.BlockSpec((r, GROUP_WIDTH), lambda i: (i, 0)),
            pl.BlockSpec((1, D_MODEL, D_MODEL), lay3, pipeline_mode=once),
            pl.BlockSpec((1, 1, N_MOD, D_MODEL), lambda i: (layer, stream, 0, 0)),
            pl.BlockSpec((1, 1, D_MODEL), lay3),
            pl.BlockSpec((1, 1, D_MODEL), lay3),
            pl.BlockSpec((1, D_MODEL, 2 * D_FF), lay3, pipeline_mode=once),
            pl.BlockSpec((1, D_FF, D_MODEL), lay3, pipeline_mode=once),
            pl.BlockSpec((1, 1, D_MODEL), lay3),
        ],
        out_specs=pl.BlockSpec((r, D_MODEL), lambda i: (i, 0)),
        out_shape=jax.ShapeDtypeStruct((t, D_MODEL), F32),
        compiler_params=_cparams(("arbitrary",)),
        name="mix_ffn",
    )(x, ot_a, ot_c, ot_d, yb, p["w_out"], mod, p["g_post1"], p["g_pre2"], p["w_gu"], p["w_down"], p["g_post2"])


def _rope_tables(n):
    rows = n // GRID_W
    row = jnp.repeat(jnp.arange(rows, dtype=F32), GRID_W)
    col = jnp.tile(jnp.arange(GRID_W, dtype=F32), rows)
    out = []
    for rot_dim in (HEAD_DIM, C_ROPE):
        n_axis = rot_dim // 4
        freqs = ROPE_THETA ** (-jnp.arange(n_axis, dtype=F32) / n_axis)
        ang = jnp.concatenate([row[:, None] * freqs, col[:, None] * freqs], axis=-1)
        out += [jnp.cos(ang).T, jnp.sin(ang).T]
    return tuple(out)


def _identity_rope(n):
    return (jnp.ones((HEAD_DIM // 2, n), F32), jnp.zeros((HEAD_DIM // 2, n), F32),
            jnp.ones((C_ROPE // 2, n), F32), jnp.zeros((C_ROPE // 2, n), F32))


def _pack_params(w_in, pool_w, w_out, w_gu, w_down, c_wuq, c_wukv):
    depth = w_in.shape[0]
    pad = jnp.zeros((depth, D_MODEL, IN_PACKED - IN_WIDTH), w_in.dtype)
    w_in_p = jnp.concatenate([w_in[:, :, :KR_END], pad, w_in[:, :, KR_END:]], axis=2).astype(BF16)
    eye = jnp.eye(len(POOL_SIZES), dtype=pool_w.dtype)
    pool_bd = jnp.einsum("lgce,gh->lgche", pool_w, eye).reshape(depth, GROUP_WIDTH, GROUP_WIDTH).astype(BF16)
    w_out_p = jnp.concatenate([w_out[:, 0:256], w_out[:, 512:1024], w_out[:, 256:512]], axis=1).astype(BF16)
    return dict(w_in=w_in_p, pool_bd=pool_bd, w_out=w_out_p, w_gu=w_gu.astype(BF16), w_down=w_down.astype(BF16),
                c_wuq=c_wuq.astype(BF16), c_wukv=c_wukv.astype(BF16))


def kernel(x, c, ctx, c_ctx, w_ada, b_ada, g_pre1, g_post1, w_in, a_sink, pool_w, pool_scale, c_gq, c_wuq, c_gkv,
           c_wukv, d_gq, d_gk, w_out, g_pre2, g_post2, w_gu, w_down):
    assert x.shape[0] == 1 and ctx.shape[0] == 1
    depth = w_in.shape[0]
    n = x.shape[1]
    n_ctx = ctx.shape[1]
    p = _pack_params(w_in, pool_w, w_out, w_gu, w_down, c_wuq, c_wukv)
    row3 = lambda a: a.reshape(depth, 1, a.shape[1])
    p.update(g_pre1=row3(g_pre1), g_post1=row3(g_post1), g_pre2=row3(g_pre2), g_post2=row3(g_post2),
             pool_scale=row3(pool_scale), c_gq=row3(c_gq), c_gkv=row3(c_gkv),
             d_gq=d_gq.reshape(depth, HEAD_DIM, 1), d_gk=d_gk.reshape(depth, HEAD_DIM, 1))

    cond = jnp.zeros((8, D_MODEL), F32).at[0].set(c[0]).at[1].set(c_ctx)
    mod = _modulation(cond, w_ada, b_ada).reshape(depth, 8, N_MOD, D_MODEL)

    rope_x = _rope_tables(n)
    rope_c = _identity_rope(n_ctx)
    xs, cs = x[0], ctx[0]
    for layer in range(depth):
        last = layer == depth - 1
        sink = a_sink[layer] * LOG2E
        qta, ka, vta, ub, qtc, kc, vtc, qtd, kd, vtd = _project(xs, mod, 0, layer, p, rope_x)
        qta_c, ka_c, vta_c, ub_c, qtc_c, kc_c, vtc_c, qtd_c, kd_c, vtd_c = _project(cs, mod, 1, layer, p, rope_c)

        ot_a = _window_attention(qta, ka_c, vta_c, ka, vta, sink)
        yb = _pool(ub, layer, p)
        probe_c = (dict(q_tile=1024, rotate=False), dict(q_tile=1024, rotate=True, s_pad=LANE))[layer]
        probe_d = (dict(q_tile=512, rotate=True), dict(q_tile=512, rotate=False, kv_block=1024))[layer]
        ot_c = _dense_attention(qtc, kc_c, vtc_c, kc, vtc, name="dense_attn_c", **probe_c)
        ot_d = _dense_attention(qtd, kd_c, vtd_c, kd, vtd, name="dense_attn_d", **probe_d)
        xs = _mix_ffn(xs, ot_a, ot_c, ot_d, yb, mod, 0, layer, p)

        if not last:
            ot_a_c = _dense_attention(qta_c, ka_c, vta_c, sink=sink, name="ctx_attn_a")
            yb_c = _pool(ub_c, layer, p)
            ot_c_c = _dense_attention(qtc_c, kc_c, vtc_c, name="ctx_attn_c")
            ot_d_c = _dense_attention(qtd_c, kd_c, vtd_c, name="ctx_attn_d")
            cs = _mix_ffn(cs, ot_a_c, ot_c_c, ot_d_c, yb_c, mod, 1, layer, p)
    return xs[None]
```

```python
import functools

import numpy as np
import jax
import jax.numpy as jnp
from jax import lax
from jax.experimental import pallas as pl
from jax.experimental.pallas import tpu as pltpu

F32 = jnp.float32
BF16 = jnp.bfloat16

D_MODEL = 1024
GRID_W = 64
GROUP_WIDTH = 256
HEAD_DIM = 64
ROPE_THETA = 10000.0
EPS = 1e-6
NEG_INF = -1e30
LOG2E = 1.4426950408889634
N_MOD = 6
N_HEADS = 4
KV_HEADS = 2
WINDOW = 128
POOL_SIZES = (2, 4, 8, 16)
POOL_GROUP = 64
C_NOPE, C_ROPE, C_V = 64, 32, 64
C_Q_RANK, C_KV_RANK = 256, 128
D_FF = 2816
IN_WIDTH = 1696
KR_END = 1184
IN_PACKED = 1792

LANE = 128
QK_PAD = 128
V_ROWS = 80
ROW_TILE = 512
SUB_TILE = 256
Q_TILE = 512
WIN_TILE = 256
WIN_STEP = 1024
FLASH_SUB_TILES = 4
KV_BLOCK = 1024
MXU_DEPTH = 256
MAX_TRIP_BLOCKS = 6
HALO = 128
POOL_HALO = 8
VMEM_LIMIT = 56 * 1024 * 1024


def _cparams(sem):
    return pltpu.CompilerParams(dimension_semantics=sem, vmem_limit_bytes=VMEM_LIMIT)


def _rms(x, g):
    ms = jnp.mean(x * x, axis=-1, keepdims=True)
    return x * lax.rsqrt(ms + EPS) * g


def _split_bf16(x):
    hi = x.astype(BF16)
    lo = (x - hi.astype(F32)).astype(BF16)
    return hi, lo


def _mod_kernel(cond_ref, w_ref, b_ref, o_ref):
    c = cond_ref[...]
    s = c * jax.nn.sigmoid(c)
    s_hi, s_lo = _split_bf16(s)
    w_hi, w_lo = _split_bf16(w_ref[0])
    acc = jnp.dot(s_hi, w_hi, preferred_element_type=F32)
    acc += jnp.dot(s_hi, w_lo, preferred_element_type=F32)
    acc += jnp.dot(s_lo, w_hi, preferred_element_type=F32)
    o_ref[0] = acc + b_ref[0]


def _modulation(cond, w_ada, b_ada):
    depth, d, n = w_ada.shape
    tn = 1536
    return pl.pallas_call(
        _mod_kernel,
        grid=(depth, n // tn),
        in_specs=[
            pl.BlockSpec((8, d), lambda l, j: (0, 0)),
            pl.BlockSpec((1, d, tn), lambda l, j: (l, 0, j)),
            pl.BlockSpec((1, 1, tn), lambda l, j: (l, 0, j)),
        ],
        out_specs=pl.BlockSpec((1, 8, tn), lambda l, j: (l, 0, j)),
        out_shape=jax.ShapeDtypeStruct((depth, 8, n), F32),
        compiler_params=_cparams(("arbitrary", "arbitrary")),
        name="modulation",
    )(cond, w_ada, b_ada.reshape(depth, 1, n))


def _rope_t(xt, cos, sin):
    half = xt.shape[0] // 2
    x1, x2 = xt[:half], xt[half:]
    return jnp.concatenate([x1 * cos - x2 * sin, x2 * cos + x1 * sin], axis=0)


def _ones_rows(r):
    row = lax.broadcasted_iota(jnp.int32, (V_ROWS - HEAD_DIM, r), 0)
    return jnp.where(row == 0, 1.0, 0.0).astype(BF16)


def _proj_kernel(x_ref, mod_ref, gpre_ref, win_ref, cgq_ref, cwuq_ref, cgkv_ref, cwukv_ref,
                 dgq_ref, dgk_ref, cos_hd_ref, sin_hd_ref, cos_c_ref, sin_c_ref,
                 qta_ref, ka_ref, vta_ref, u_ref, qtc_ref, kc_ref, vtc_ref, qtd_ref, kd_ref, vtd_ref):
    subs = [slice(k, k + SUB_TILE) for k in range(0, x_ref.shape[0], SUB_TILE)]
    zs = []
    for sl in subs:
        h = _rms(x_ref[sl, :], gpre_ref[0])
        h = h * (1.0 + mod_ref[0, 0, 1:2, :]) + mod_ref[0, 0, 0:1, :]
        zs.append(jnp.dot(h.astype(BF16), win_ref[0], preferred_element_type=F32))
    for sl, z in zip(subs, zs):
        _proj_groups(z, sl, cgq_ref, cwuq_ref, cgkv_ref, cwukv_ref, dgq_ref, dgk_ref,
                     cos_hd_ref, sin_hd_ref, cos_c_ref, sin_c_ref,
                     qta_ref, ka_ref, vta_ref, u_ref, qtc_ref, kc_ref, vtc_ref, qtd_ref, kd_ref, vtd_ref)


def _proj_groups(z, sl, cgq_ref, cwuq_ref, cgkv_ref, cwukv_ref, dgq_ref, dgk_ref,
                 cos_hd_ref, sin_hd_ref, cos_c_ref, sin_c_ref,
                 qta_ref, ka_ref, vta_ref, u_ref, qtc_ref, kc_ref, vtc_ref, qtd_ref, kd_ref, vtd_ref):
    r = z.shape[0]
    scale_hd = HEAD_DIM ** -0.5 * LOG2E
    scale_c = (C_NOPE + C_ROPE) ** -0.5 * LOG2E
    cos_hd, sin_hd = cos_hd_ref[:, sl], sin_hd_ref[:, sl]
    cos_c, sin_c = cos_c_ref[:, sl], sin_c_ref[:, sl]
    ones = _ones_rows(r)
    zeros64 = jnp.zeros((HEAD_DIM, r), BF16)

    def put_gqa_q(ref, qt, hd):
        g = hd // (N_HEADS // KV_HEADS)
        ref[hd, g * HEAD_DIM:(g + 1) * HEAD_DIM, sl] = (qt * scale_hd).astype(BF16)
        ref[hd, (1 - g) * HEAD_DIM:(2 - g) * HEAD_DIM, sl] = zeros64

    def put_v(ref, hd, vt):
        ref[hd, 0:HEAD_DIM, sl] = vt.astype(BF16)
        ref[hd, HEAD_DIM:V_ROWS, sl] = ones

    qa_t = z[:, 0:256].T
    for hd in range(N_HEADS):
        put_gqa_q(qta_ref, _rope_t(qa_t[hd * 64:(hd + 1) * 64], cos_hd, sin_hd), hd)
    ka_t = z[:, 256:384].T
    ka_t = jnp.concatenate([_rope_t(ka_t[g * 64:(g + 1) * 64], cos_hd, sin_hd) for g in range(KV_HEADS)], axis=0)
    ka_ref[0, sl, :] = ka_t.T.astype(BF16)
    va_t = z[:, 384:512].T
    for g in range(KV_HEADS):
        put_v(vta_ref, g, va_t[g * 64:(g + 1) * 64])

    u_ref[sl, :] = z[:, 512:768]

    cq = _rms(z[:, 768:1024], cgq_ref[0])
    qc_t = jnp.dot(cq.astype(BF16), cwuq_ref[0], preferred_element_type=F32).T
    ckv = _rms(z[:, 1024:1152], cgkv_ref[0])
    kvc_t = jnp.dot(ckv.astype(BF16), cwukv_ref[0], preferred_element_type=F32).T
    kr_t = _rope_t(z[:, 1152:1280].T[0:C_ROPE], cos_c, sin_c)
    pad32 = jnp.zeros((QK_PAD - C_NOPE - C_ROPE, r), F32)
    per_head_q = C_NOPE + C_ROPE
    for hd in range(N_HEADS):
        q_nope = qc_t[hd * per_head_q:hd * per_head_q + C_NOPE]
        q_rope = _rope_t(qc_t[hd * per_head_q + C_NOPE:(hd + 1) * per_head_q], cos_c, sin_c)
        qtc_ref[hd, :, sl] = (jnp.concatenate([q_nope, q_rope, pad32], axis=0) * scale_c).astype(BF16)
        k_nope = kvc_t[hd * 128:hd * 128 + C_NOPE]
        kc_ref[hd, sl, :] = jnp.concatenate([k_nope, kr_t, pad32], axis=0).T.astype(BF16)
        put_v(vtc_ref, hd, kvc_t[hd * 128 + C_NOPE:(hd + 1) * 128])

    def head_norm(xt, g_col):
        ms = jnp.mean(xt * xt, axis=0, keepdims=True)
        return xt * lax.rsqrt(ms + EPS) * g_col

    qd_t = z[:, 1280:1536].T
    for hd in range(N_HEADS):
        q = head_norm(qd_t[hd * 64:(hd + 1) * 64], dgq_ref[0])
        put_gqa_q(qtd_ref, _rope_t(q, cos_hd, sin_hd), hd)
    kd_t = z[:, 1536:1664].T
    kd_t = jnp.concatenate(
        [_rope_t(head_norm(kd_t[g * 64:(g + 1) * 64], dgk_ref[0]), cos_hd, sin_hd) for g in range(KV_HEADS)], axis=0)
    kd_ref[0, sl, :] = kd_t.T.astype(BF16)
    vd_t = z[:, 1664:1792].T
    for g in range(KV_HEADS):
        put_v(vtd_ref, g, vd_t[g * 64:(g + 1) * 64])


def _project(x, mod, stream, layer, p, rope):
    t = x.shape[0]
    r = min(ROW_TILE, t)
    cos_hd, sin_hd, cos_c, sin_c = rope
    lay3 = lambda i: (layer, 0, 0)
    tok = lambda i: (i, 0)
    feat3 = lambda i: (0, 0, i)
    tok3 = lambda i: (0, i, 0)
    out_shapes = (
        jax.ShapeDtypeStruct((N_HEADS, QK_PAD, t), BF16),
        jax.ShapeDtypeStruct((1, t, QK_PAD), BF16),
        jax.ShapeDtypeStruct((KV_HEADS, V_ROWS, t), BF16),
        jax.ShapeDtypeStruct((t, GROUP_WIDTH), F32),
        jax.ShapeDtypeStruct((N_HEADS, QK_PAD, t), BF16),
        jax.ShapeDtypeStruct((N_HEADS, t, QK_PAD), BF16),
        jax.ShapeDtypeStruct((N_HEADS, V_ROWS, t), BF16),
        jax.ShapeDtypeStruct((N_HEADS, QK_PAD, t), BF16),
        jax.ShapeDtypeStruct((1, t, QK_PAD), BF16),
        jax.ShapeDtypeStruct((KV_HEADS, V_ROWS, t), BF16),
    )
    out_specs = (
        pl.BlockSpec((N_HEADS, QK_PAD, r), feat3),
        pl.BlockSpec((1, r, QK_PAD), tok3),
        pl.BlockSpec((KV_HEADS, V_ROWS, r), feat3),
        pl.BlockSpec((r, GROUP_WIDTH), tok),
        pl.BlockSpec((N_HEADS, QK_PAD, r), feat3),
        pl.BlockSpec((N_HEADS, r, QK_PAD), tok3),
        pl.BlockSpec((N_HEADS, V_ROWS, r), feat3),
        pl.BlockSpec((N_HEADS, QK_PAD, r), feat3),
        pl.BlockSpec((1, r, QK_PAD), tok3),
        pl.BlockSpec((KV_HEADS, V_ROWS, r), feat3),
    )
    in_specs = [
        pl.BlockSpec((r, D_MODEL), tok),
        pl.BlockSpec((1, 1, N_MOD, D_MODEL), lambda i: (layer, stream, 0, 0)),
        pl.BlockSpec((1, 1, D_MODEL), lay3),
        pl.BlockSpec((1, D_MODEL, IN_PACKED), lay3),
        pl.BlockSpec((1, 1, C_Q_RANK), lay3),
        pl.BlockSpec((1, C_Q_RANK, p["c_wuq"].shape[2]), lay3),
        pl.BlockSpec((1, 1, C_KV_RANK), lay3),
        pl.BlockSpec((1, C_KV_RANK, p["c_wukv"].shape[2]), lay3),
        pl.BlockSpec((1, HEAD_DIM, 1), lay3),
        pl.BlockSpec((1, HEAD_DIM, 1), lay3),
        pl.BlockSpec((HEAD_DIM // 2, r), lambda i: (0, i)),
        pl.BlockSpec((HEAD_DIM // 2, r), lambda i: (0, i)),
        pl.BlockSpec((C_ROPE // 2, r), lambda i: (0, i)),
        pl.BlockSpec((C_ROPE // 2, r), lambda i: (0, i)),
    ]
    return pl.pallas_call(
        _proj_kernel,
        grid=(t // r,),
        in_specs=in_specs,
        out_specs=out_specs,
        out_shape=out_shapes,
        compiler_params=_cparams(("arbitrary",)),
        name="in_proj",
    )(x, mod, p["g_pre1"], p["w_in"], p["c_gq"], p["c_wuq"], p["c_gkv"], p["c_wukv"],
      p["d_gq"], p["d_gk"], cos_hd, sin_hd, cos_c, sin_c)


def _trip_blocks(n_blocks):
    return max(b for b in range(2, MAX_TRIP_BLOCKS + 1, 2) if (n_blocks - 2) % b == 0)


def _flash_tile(q, sink, kc_ref, vtc_ref, kl_ref, vtl_ref, m_ref, acc_ref, scratch, n_blocks, emit):
    s = jnp.dot(kc_ref[0], q, preferred_element_type=F32)

    if n_blocks:
        s_refs, cm_refs, p_refs, al_refs = scratch[0:2], scratch[2:4], scratch[4:6], scratch[6:8]
        parts = KV_BLOCK // MXU_DEPTH

        def produce_part(slot, j, h, cm):
            off = pl.multiple_of(j * KV_BLOCK + h * MXU_DEPTH, MXU_DEPTH)
            sj = jnp.dot(kl_ref[0, pl.ds(off, MXU_DEPTH), :], q, preferred_element_type=F32)
            s_refs[slot][h * MXU_DEPTH:(h + 1) * MXU_DEPTH, :] = sj
            cmh = jnp.max(sj, axis=0, keepdims=True)
            return cmh if cm is None else jnp.maximum(cm, cmh)

        def soften(slot):
            m_old = m_ref[...]
            m_new = jnp.maximum(m_old, cm_refs[slot][...])
            al_refs[slot][...] = jnp.exp2(m_old - m_new)
            p_refs[slot][...] = jnp.exp2(s_refs[slot][...] - m_new).astype(BF16)
            m_ref[...] = m_new

        def accumulate_part(slot, j, h):
            off = pl.multiple_of(j * KV_BLOCK + h * MXU_DEPTH, MXU_DEPTH)
            pv = jnp.dot(vtl_ref[0, :, pl.ds(off, MXU_DEPTH)], p_refs[slot][h * MXU_DEPTH:(h + 1) * MXU_DEPTH, :],
                         preferred_element_type=F32)
            scale = al_refs[slot][...] if h == 0 else 1.0
            acc_ref[...] = scale * acc_ref[...] + pv

        def stages(slot_p, j_p, slot_a, j_a):
            cm = None
            for h in range(parts):
                if j_p is not None:
                    cm = produce_part(slot_p, j_p, h, cm)
                if j_a is not None:
                    accumulate_part(slot_a, j_a, h)
            if j_p is not None:
                cm_refs[slot_p][...] = cm

        stages(0, 0, None, None)
    yield

    m = jnp.max(s, axis=0, keepdims=True)
    if sink is not None:
        m = jnp.maximum(m, sink)
    p = jnp.exp2(s - m)
    acc = jnp.dot(vtc_ref[0], p.astype(BF16), preferred_element_type=F32)
    if sink is not None:
        row = lax.broadcasted_iota(jnp.int32, acc.shape, 0)
        acc = acc + jnp.where(row == HEAD_DIM, jnp.exp2(sink - m), 0.0)

    if n_blocks:
        m_ref[...] = m
        acc_ref[...] = acc
        p_refs[1][...] = jnp.zeros(p_refs[1].shape, BF16)
        al_refs[1][...] = jnp.ones(al_refs[1].shape, F32)
        per_trip = _trip_blocks(n_blocks)
        stages(1, 1, None, None)
    yield

    if n_blocks:
        def trip(t, carry):
            for b in range(per_trip):
                j = t * per_trip + b
                soften(b % 2)
                stages(b % 2, j + 2, 1 - b % 2, jnp.maximum(j - 1, 0))
            return carry

        lax.fori_loop(0, (n_blocks - 2) // per_trip, trip, 0)
    yield

    for j in (n_blocks - 2, n_blocks - 1):
        if n_blocks:
            soften(j % 2)
            stages(None, None, 1 - j % 2, j - 1)
        yield

    if n_blocks:
        stages(None, None, (n_blocks - 1) % 2, n_blocks - 1)
        acc = acc_ref[...]
    emit(acc)
    yield


def _flash_kernel(*refs, n_blocks, has_sink, n_sub):
    refs = list(refs)
    sink = refs.pop(0)[pl.program_id(0)] if has_sink else None
    qt_ref, kc_ref, vtc_ref = refs[:3]
    kl_ref, vtl_ref = refs[3:5] if n_blocks else (None, None)
    refs = refs[5:] if n_blocks else refs[3:]
    o_ref, scratch = refs[0], refs[1:]
    per_sub = len(scratch) // n_sub
    tq = o_ref.shape[1] // n_sub

    def tile(k):
        def emit(acc):
            o_ref[:, k * tq:(k + 1) * tq] = (acc[0:HEAD_DIM] / acc[HEAD_DIM:HEAD_DIM + 1]).astype(o_ref.dtype)

        sub = scratch[k * per_sub:(k + 1) * per_sub]
        return _flash_tile(qt_ref[0, :, k * tq:(k + 1) * tq], sink, kc_ref, vtc_ref, kl_ref, vtl_ref,
                           sub[0], sub[1], sub[2:], n_blocks, emit)

    tiles = [tile(k) for k in range(n_sub)]
    for _ in range(3):
        next(tiles[0])
    for k in range(n_sub):
        follower = tiles[k + 1] if k + 1 < n_sub else None
        for _ in range(2):
            if follower is not None:
                next(follower)
            next(tiles[k])
        next(tiles[k])
        if follower is not None:
            next(follower)


def _dense_attention(qt, k_ctx, vt_ctx, k_lat=None, vt_lat=None, sink=None, name="dense_attn"):
    n_heads, _, tq_total = qt.shape
    tq = min(Q_TILE, tq_total)
    n_sub = min(FLASH_SUB_TILES, tq_total // tq)
    step = tq * n_sub
    n_ctx = k_ctx.shape[1]
    k_rep = n_heads // k_ctx.shape[0]
    v_rep = n_heads // vt_ctx.shape[0]
    has_sink = sink is not None
    has_lat = k_lat is not None
    n_lat = k_lat.shape[1] if has_lat else 0
    n_blocks = n_lat // KV_BLOCK
    assert n_lat % (2 * KV_BLOCK) == 0 and n_blocks != 2
    args, in_specs = [], []
    if has_sink:
        args.append(sink)
        in_specs.append(pl.BlockSpec(memory_space=pltpu.SMEM))
    args += [qt, k_ctx, vt_ctx]
    in_specs += [
        pl.BlockSpec((1, QK_PAD, step), lambda h, i: (h, 0, i)),
        pl.BlockSpec((1, n_ctx, QK_PAD), lambda h, i: (h // k_rep, 0, 0)),
        pl.BlockSpec((1, V_ROWS, n_ctx), lambda h, i: (h // v_rep, 0, 0)),
    ]
    scratch = [pltpu.VMEM((1, tq), F32), pltpu.VMEM((V_ROWS, tq), F32)]
    if has_lat:
        args += [k_lat, vt_lat]
        in_specs += [
            pl.BlockSpec((1, n_lat, QK_PAD), lambda h, i: (h // k_rep, 0, 0)),
            pl.BlockSpec((1, V_ROWS, n_lat), lambda h, i: (h // v_rep, 0, 0)),
        ]
        scratch += (2 * [pltpu.VMEM((KV_BLOCK, tq), F32)] + 2 * [pltpu.VMEM((1, tq), F32)]
                    + 2 * [pltpu.VMEM((KV_BLOCK, tq), BF16)] + 2 * [pltpu.VMEM((1, tq), F32)])
    return pl.pallas_call(
        functools.partial(_flash_kernel, n_blocks=n_blocks, has_sink=has_sink, n_sub=n_sub),
        grid=(n_heads, tq_total // step),
        in_specs=in_specs,
        out_specs=pl.BlockSpec((HEAD_DIM, step), lambda h, i: (h, i)),
        out_shape=jax.ShapeDtypeStruct((n_heads * HEAD_DIM, tq_total), BF16),
        scratch_shapes=scratch * n_sub,
        compiler_params=_cparams(("arbitrary", "arbitrary")),
        name=name,
    )(*args)


def _window_kernel(sink_ref, qt_ref, kc_ref, vtc_ref, kp_ref, km_ref, kn_ref, vp_ref, vm_ref, vn_ref,
                   bias_ref, o_ref):
    i = pl.program_id(0)
    n = pl.num_programs(0)
    n_heads, n_kv = qt_ref.shape[0], vtc_ref.shape[0]
    per_kv = n_heads // n_kv
    n_tiles = WIN_STEP // WIN_TILE
    band = WIN_TILE + 2 * HALO
    dot = functools.partial(jnp.dot, preferred_element_type=F32)
    k_all = jnp.concatenate([kp_ref[0], km_ref[0], kn_ref[0]], axis=0)
    v_all = [jnp.concatenate([vp_ref[g], vm_ref[g], vn_ref[g]], axis=1) for g in range(n_kv)]
    lane_head = lax.broadcasted_iota(jnp.int32, (1, n_heads * WIN_TILE), 1) // WIN_TILE
    sink = functools.reduce(lambda acc, hd: jnp.where(lane_head == hd, sink_ref[hd], acc), range(n_heads),
                            jnp.zeros((1, n_heads * WIN_TILE), F32))
    row = lax.broadcasted_iota(jnp.int32, (band, 1), 0)

    def scores(t):
        q = jnp.concatenate([qt_ref[hd, :, t * WIN_TILE:(t + 1) * WIN_TILE] for hd in range(n_heads)], axis=1)
        s_band = dot(k_all[t * WIN_TILE:t * WIN_TILE + band], q) + bias_ref[...]
        if t == 0:
            s_band = s_band + jnp.where((row < HALO) & (i == 0), NEG_INF, 0.0)
        if t == n_tiles - 1:
            s_band = s_band + jnp.where((row >= band - HALO) & (i == n - 1), NEG_INF, 0.0)
        return dot(kc_ref[0], q), s_band

    def finish(t, ss):
        s_ctx, s_band = ss
        m = jnp.maximum(jnp.max(s_ctx, axis=0, keepdims=True), jnp.max(s_band, axis=0, keepdims=True))
        m = jnp.maximum(m, sink)
        p_ctx = jnp.exp2(s_ctx - m).astype(BF16)
        p_band = jnp.exp2(s_band - m).astype(BF16)
        sink_term = jnp.exp2(sink - m)
        for g in range(n_kv):
            cols = slice(g * per_kv * WIN_TILE, (g + 1) * per_kv * WIN_TILE)
            acc = (dot(vtc_ref[g], p_ctx[:, cols])
                   + dot(v_all[g][:, t * WIN_TILE:t * WIN_TILE + band], p_band[:, cols]))
            out = acc[0:HEAD_DIM] / (acc[HEAD_DIM:HEAD_DIM + 1] + sink_term[:, cols])
            for k in range(per_kv):
                hd = g * per_kv + k
                o_ref[hd * HEAD_DIM:(hd + 1) * HEAD_DIM, t * WIN_TILE:(t + 1) * WIN_TILE] = (
                    out[:, k * WIN_TILE:(k + 1) * WIN_TILE].astype(o_ref.dtype))

    ss = scores(0)
    for t in range(n_tiles):
        nxt = scores(t + 1) if t + 1 < n_tiles else None
        finish(t, ss)
        ss = nxt


def _window_bias(n_heads):
    r = np.arange(WIN_TILE + 2 * HALO)[:, None] - HALO
    c = np.arange(WIN_TILE)[None, :]
    bias = np.where(np.abs(c - r) <= WINDOW, 0.0, NEG_INF).astype(np.float32)
    return np.tile(bias, (1, n_heads))


def _window_attention(qt, k_ctx, vt_ctx, k_lat, vt_lat, sink):
    n_heads, _, t = qt.shape
    n_ctx = k_ctx.shape[1]
    n_kv = vt_ctx.shape[0]
    per = WIN_STEP // HALO
    last = t // HALO - 1
    prev_blk = lambda i: jnp.maximum(i * per - 1, 0)
    next_blk = lambda i: jnp.minimum(i * per + per, last)
    in_specs = [
        pl.BlockSpec(memory_space=pltpu.SMEM),
        pl.BlockSpec((n_heads, QK_PAD, WIN_STEP), lambda i: (0, 0, i)),
        pl.BlockSpec((1, n_ctx, QK_PAD), lambda i: (0, 0, 0)),
        pl.BlockSpec((n_kv, V_ROWS, n_ctx), lambda i: (0, 0, 0)),
        pl.BlockSpec((1, HALO, QK_PAD), lambda i: (0, prev_blk(i), 0)),
        pl.BlockSpec((1, WIN_STEP, QK_PAD), lambda i: (0, i, 0)),
        pl.BlockSpec((1, HALO, QK_PAD), lambda i: (0, next_blk(i), 0)),
        pl.BlockSpec((n_kv, V_ROWS, HALO), lambda i: (0, 0, prev_blk(i))),
        pl.BlockSpec((n_kv, V_ROWS, WIN_STEP), lambda i: (0, 0, i)),
        pl.BlockSpec((n_kv, V_ROWS, HALO), lambda i: (0, 0, next_blk(i))),
        pl.BlockSpec((WIN_TILE + 2 * HALO, n_heads * WIN_TILE), lambda i: (0, 0)),
    ]
    return pl.pallas_call(
        _window_kernel,
        grid=(t // WIN_STEP,),
        in_specs=in_specs,
        out_specs=pl.BlockSpec((n_heads * HEAD_DIM, WIN_STEP), lambda i: (0, i)),
        out_shape=jax.ShapeDtypeStruct((n_heads * HEAD_DIM, t), BF16),
        compiler_params=_cparams(("arbitrary",)),
        name="window_attn",
    )(sink, qt, k_ctx, vt_ctx, k_lat, k_lat, k_lat, vt_lat, vt_lat, vt_lat, jnp.asarray(_window_bias(n_heads)))


def _pool_kernel(u_ref, up_ref, un_ref, w_ref, scale_ref, o_ref, *, t_total):
    i = pl.program_id(0)
    n = pl.num_programs(0)
    r = u_ref.shape[0]
    u = u_ref[...]
    prev = jnp.where(i > 0, up_ref[...], 0.0)
    nxt = jnp.where(i < n - 1, un_ref[...], 0.0)
    ext = jnp.concatenate([prev, u, nxt], axis=0)
    rows = ext.shape[0]

    def shifted(x, k):
        return pltpu.roll(x, k % rows, axis=0)

    def window_sums(e):
        s2 = shifted(e, 1) + e
        s4 = shifted(s2, 1) + shifted(s2, -1)
        s8 = shifted(s4, 2) + shifted(s4, -2)
        s16 = shifted(s8, 4) + shifted(s8, -4)
        return s2, s4, s8, s16

    low_group = lax.broadcasted_iota(jnp.int32, (rows, LANE), 1) < POOL_GROUP
    s2, s4, _, _ = window_sums(ext[:, 0:LANE])
    _, _, s8, s16 = window_sums(ext[:, LANE:2 * LANE])
    total = jnp.concatenate([jnp.where(low_group, s2, s4), jnp.where(low_group, s8, s16)], axis=1)
    total = total[POOL_HALO:POOL_HALO + r]

    lane = lax.broadcasted_iota(jnp.int32, (r, GROUP_WIDTH), 1)
    pos = lax.broadcasted_iota(jnp.int32, (r, GROUP_WIDTH), 0) + i * r
    half = jnp.zeros((r, GROUP_WIDTH), jnp.int32)
    for gi, size in enumerate(POOL_SIZES):
        half = jnp.where((lane >= gi * POOL_GROUP) & (lane < (gi + 1) * POOL_GROUP), size // 2, half)
    count = jnp.minimum(pos + half, t_total) - jnp.maximum(pos - half, 0)
    y = total / count.astype(F32) - u
    o_ref[...] = (jnp.dot(y.astype(BF16), w_ref[0], preferred_element_type=F32) * scale_ref[0]).astype(o_ref.dtype)


def _pool(u, layer, p):
    t = u.shape[0]
    r = min(ROW_TILE, t)
    per = r // POOL_HALO
    last = t // POOL_HALO - 1
    return pl.pallas_call(
        functools.partial(_pool_kernel, t_total=t),
        grid=(t // r,),
        in_specs=[
            pl.BlockSpec((r, GROUP_WIDTH), lambda i: (i, 0)),
            pl.BlockSpec((POOL_HALO, GROUP_WIDTH), lambda i: (jnp.maximum(i * per - 1, 0), 0)),
            pl.BlockSpec((POOL_HALO, GROUP_WIDTH), lambda i: (jnp.minimum(i * per + per, last), 0)),
            pl.BlockSpec((1, GROUP_WIDTH, GROUP_WIDTH), lambda i: (layer, 0, 0)),
            pl.BlockSpec((1, 1, GROUP_WIDTH), lambda i: (layer, 0, 0)),
        ],
        out_specs=pl.BlockSpec((r, GROUP_WIDTH), lambda i: (i, 0)),
        out_shape=jax.ShapeDtypeStruct((t, GROUP_WIDTH), BF16),
        compiler_params=_cparams(("arbitrary",)),
        name="pool",
    )(u, u, u, p["pool_bd"], p["pool_scale"])


def _mix_ffn_kernel(x_ref, ota_ref, otc_ref, otd_ref, yb_ref, wout_ref, mod_ref, gpost1_ref,
                    gpre2_ref, wgu_ref, wdown_ref, gpost2_ref, o_ref):
    subs = [slice(k, k + SUB_TILE) for k in range(0, x_ref.shape[0], SUB_TILE)]
    n_att = ota_ref.shape[0] + otc_ref.shape[0] + otd_ref.shape[0]
    ys = []
    for sl in subs:
        ot = jnp.concatenate([ota_ref[:, sl], otc_ref[:, sl], otd_ref[:, sl]], axis=0)
        y = lax.dot_general(ot, wout_ref[0, 0:n_att, :], (((0,), (0,)), ((), ())), preferred_element_type=F32)
        ys.append(y + jnp.dot(yb_ref[sl, :], wout_ref[0, n_att:, :], preferred_element_type=F32))
    xs = [x_ref[sl, :] + mod_ref[0, 0, 2:3, :] * _rms(y, gpost1_ref[0]) for sl, y in zip(subs, ys)]
    gus = []
    for x in xs:
        h = _rms(x, gpre2_ref[0])
        h = h * (1.0 + mod_ref[0, 0, 4:5, :]) + mod_ref[0, 0, 3:4, :]
        gus.append(jnp.dot(h.astype(BF16), wgu_ref[0], preferred_element_type=F32))
    y2s = []
    for gu in gus:
        gate, up = gu[:, :D_FF], gu[:, D_FF:]
        act = (gate * jax.nn.sigmoid(gate) * up).astype(BF16)
        y2s.append(jnp.dot(act, wdown_ref[0], preferred_element_type=F32))
    for sl, x, y2 in zip(subs, xs, y2s):
        o_ref[sl, :] = x + mod_ref[0, 0, 5:6, :] * _rms(y2, gpost2_ref[0])


def _mix_ffn(x, ot_a, ot_c, ot_d, yb, mod, stream, layer, p):
    t = x.shape[0]
    r = min(ROW_TILE, t)
    feat = pl.BlockSpec((GROUP_WIDTH, r), lambda i: (0, i))
    lay3 = lambda i: (layer, 0, 0)
    once = pl.Buffered(1)
    return pl.pallas_call(
        _mix_ffn_kernel,
        grid=(t // r,),
        in_specs=[
            pl.BlockSpec((r, D_MODEL), lambda i: (i, 0)),
            feat, feat, feat,
            pl.BlockSpec((r, GROUP_WIDTH), lambda i: (i, 0)),
            pl.BlockSpec((1, D_MODEL, D_MODEL), lay3, pipeline_mode=once),
            pl.BlockSpec((1, 1, N_MOD, D_MODEL), lambda i: (layer, stream, 0, 0)),
            pl.BlockSpec((1, 1, D_MODEL), lay3),
            pl.BlockSpec((1, 1, D_MODEL), lay3),
            pl.BlockSpec((1, D_MODEL, 2 * D_FF), lay3, pipeline_mode=once),
            pl.BlockSpec((1, D_FF, D_MODEL), lay3, pipeline_mode=once),
            pl.BlockSpec((1, 1, D_MODEL), lay3),
        ],
        out_specs=pl.BlockSpec((r, D_MODEL), lambda i: (i, 0)),
        out_shape=jax.ShapeDtypeStruct((t, D_MODEL), F32),
        compiler_params=_cparams(("arbitrary",)),
        name="mix_ffn",
    )(x, ot_a, ot_c, ot_d, yb, p["w_out"], mod, p["g_post1"], p["g_pre2"], p["w_gu"], p["w_down"], p["g_post2"])


def _rope_tables(n):
    rows = n // GRID_W
    row = jnp.repeat(jnp.arange(rows, dtype=F32), GRID_W)
    col = jnp.tile(jnp.arange(GRID_W, dtype=F32), rows)
    out = []
    for rot_dim in (HEAD_DIM, C_ROPE):
        n_axis = rot_dim // 4
        freqs = ROPE_THETA ** (-jnp.arange(n_axis, dtype=F32) / n_axis)
        ang = jnp.concatenate([row[:, None] * freqs, col[:, None] * freqs], axis=-1)
        out += [jnp.cos(ang).T, jnp.sin(ang).T]
    return tuple(out)


def _identity_rope(n):
    return (jnp.ones((HEAD_DIM // 2, n), F32), jnp.zeros((HEAD_DIM // 2, n), F32),
            jnp.ones((C_ROPE // 2, n), F32), jnp.zeros((C_ROPE // 2, n), F32))


def _pack_params(w_in, pool_w, w_out, w_gu, w_down, c_wuq, c_wukv):
    depth = w_in.shape[0]
    pad = jnp.zeros((depth, D_MODEL, IN_PACKED - IN_WIDTH), w_in.dtype)
    w_in_p = jnp.concatenate([w_in[:, :, :KR_END], pad, w_in[:, :, KR_END:]], axis=2).astype(BF16)
    eye = jnp.eye(len(POOL_SIZES), dtype=pool_w.dtype)
    pool_bd = jnp.einsum("lgce,gh->lgche", pool_w, eye).reshape(depth, GROUP_WIDTH, GROUP_WIDTH).astype(BF16)
    w_out_p = jnp.concatenate([w_out[:, 0:256], w_out[:, 512:1024], w_out[:, 256:512]], axis=1).astype(BF16)
    return dict(w_in=w_in_p, pool_bd=pool_bd, w_out=w_out_p, w_gu=w_gu.astype(BF16), w_down=w_down.astype(BF16),
                c_wuq=c_wuq.astype(BF16), c_wukv=c_wukv.astype(BF16))


def kernel(x, c, ctx, c_ctx, w_ada, b_ada, g_pre1, g_post1, w_in, a_sink, pool_w, pool_scale, c_gq, c_wuq, c_gkv,
           c_wukv, d_gq, d_gk, w_out, g_pre2, g_post2, w_gu, w_down):
    assert x.shape[0] == 1 and ctx.shape[0] == 1
    depth = w_in.shape[0]
    n = x.shape[1]
    n_ctx = ctx.shape[1]
    p = _pack_params(w_in, pool_w, w_out, w_gu, w_down, c_wuq, c_wukv)
    row3 = lambda a: a.reshape(depth, 1, a.shape[1])
    p.update(g_pre1=row3(g_pre1), g_post1=row3(g_post1), g_pre2=row3(g_pre2), g_post2=row3(g_post2),
             pool_scale=row3(pool_scale), c_gq=row3(c_gq), c_gkv=row3(c_gkv),
             d_gq=d_gq.reshape(depth, HEAD_DIM, 1), d_gk=d_gk.reshape(depth, HEAD_DIM, 1))

    cond = jnp.zeros((8, D_MODEL), F32).at[0].set(c[0]).at[1].set(c_ctx)
    mod = _modulation(cond, w_ada, b_ada).reshape(depth, 8, N_MOD, D_MODEL)

    rope_x = _rope_tables(n)
    rope_c = _identity_rope(n_ctx)
    xs, cs = x[0], ctx[0]
    for layer in range(depth):
        last = layer == depth - 1
        sink = a_sink[layer] * LOG2E
        qta, ka, vta, ub, qtc, kc, vtc, qtd, kd, vtd = _project(xs, mod, 0, layer, p, rope_x)
        qta_c, ka_c, vta_c, ub_c, qtc_c, kc_c, vtc_c, qtd_c, kd_c, vtd_c = _project(cs, mod, 1, layer, p, rope_c)

        ot_a = _window_attention(qta, ka_c, vta_c, ka, vta, sink)
        yb = _pool(ub, layer, p)
        ot_c = _dense_attention(qtc, kc_c, vtc_c, kc, vtc, name="dense_attn_c")
        ot_d = _dense_attention(qtd, kd_c, vtd_c, kd, vtd, name="dense_attn_d")
        xs = _mix_ffn(xs, ot_a, ot_c, ot_d, yb, mod, 0, layer, p)

        if not last:
            ot_a_c = _dense_attention(qta_c, ka_c, vta_c, sink=sink, name="ctx_attn_a")
            yb_c = _pool(ub_c, layer, p)
            ot_c_c = _dense_attention(qtc_c, kc_c, vtc_c, name="ctx_attn_c")
            ot_d_c = _dense_attention(qtd_c, kd_c, vtd_c, name="ctx_attn_d")
            cs = _mix_ffn(cs, ot_a_c, ot_c_c, ot_d_c, yb_c, mod, 1, layer, p)
    return xs[None]
```

```python
import functools

import numpy as np
import jax
import jax.numpy as jnp
from jax import lax
from jax.experimental import pallas as pl
from jax.experimental.pallas import tpu as pltpu

F32 = jnp.float32
BF16 = jnp.bfloat16

D_MODEL = 1024
GRID_W = 64
GROUP_WIDTH = 256
HEAD_DIM = 64
ROPE_THETA = 10000.0
EPS = 1e-6
NEG_INF = -1e30
LOG2E = 1.4426950408889634
N_MOD = 6
N_HEADS = 4
KV_HEADS = 2
WINDOW = 128
POOL_SIZES = (2, 4, 8, 16)
POOL_GROUP = 64
C_NOPE, C_ROPE, C_V = 64, 32, 64
C_Q_RANK, C_KV_RANK = 256, 128
D_FF = 2816
IN_WIDTH = 1696
KR_END = 1184
IN_PACKED = 1792

LANE = 128
QK_PAD = 128
V_ROWS = 80
ROW_TILE = 512
SUB_TILE = 256
Q_TILE = 512
WIN_TILE = 256
WIN_STEP = 1024
FLASH_GROUP = 2
FLASH_SUB_TILES = 4
KV_BLOCK = 1024
MXU_DEPTH = 256
MAX_TRIP_BLOCKS = 2
HALO = 128
POOL_HALO = 8
VMEM_LIMIT = 56 * 1024 * 1024


def _cparams(sem):
    return pltpu.CompilerParams(dimension_semantics=sem, vmem_limit_bytes=VMEM_LIMIT)


def _rms(x, g):
    ms = jnp.mean(x * x, axis=-1, keepdims=True)
    return x * lax.rsqrt(ms + EPS) * g


def _split_bf16(x):
    hi = x.astype(BF16)
    lo = (x - hi.astype(F32)).astype(BF16)
    return hi, lo


def _mod_kernel(cond_ref, w_ref, b_ref, o_ref):
    c = cond_ref[...]
    s = c * jax.nn.sigmoid(c)
    s_hi, s_lo = _split_bf16(s)
    w_hi, w_lo = _split_bf16(w_ref[0])
    acc = jnp.dot(s_hi, w_hi, preferred_element_type=F32)
    acc += jnp.dot(s_hi, w_lo, preferred_element_type=F32)
    acc += jnp.dot(s_lo, w_hi, preferred_element_type=F32)
    o_ref[0] = acc + b_ref[0]


def _modulation(cond, w_ada, b_ada):
    depth, d, n = w_ada.shape
    tn = 1536
    return pl.pallas_call(
        _mod_kernel,
        grid=(depth, n // tn),
        in_specs=[
            pl.BlockSpec((8, d), lambda l, j: (0, 0)),
            pl.BlockSpec((1, d, tn), lambda l, j: (l, 0, j)),
            pl.BlockSpec((1, 1, tn), lambda l, j: (l, 0, j)),
        ],
        out_specs=pl.BlockSpec((1, 8, tn), lambda l, j: (l, 0, j)),
        out_shape=jax.ShapeDtypeStruct((depth, 8, n), F32),
        compiler_params=_cparams(("arbitrary", "arbitrary")),
        name="modulation",
    )(cond, w_ada, b_ada.reshape(depth, 1, n))


def _rope_t(xt, cos, sin):
    half = xt.shape[0] // 2
    x1, x2 = xt[:half], xt[half:]
    return jnp.concatenate([x1 * cos - x2 * sin, x2 * cos + x1 * sin], axis=0)


def _ones_rows(r):
    row = lax.broadcasted_iota(jnp.int32, (V_ROWS - HEAD_DIM, r), 0)
    return jnp.where(row == 0, 1.0, 0.0).astype(BF16)


def _proj_kernel(x_ref, mod_ref, gpre_ref, win_ref, cgq_ref, cwuq_ref, cgkv_ref, cwukv_ref,
                 dgq_ref, dgk_ref, cos_hd_ref, sin_hd_ref, cos_c_ref, sin_c_ref,
                 qta_ref, ka_ref, vta_ref, u_ref, qtc_ref, kc_ref, vtc_ref, qtd_ref, kd_ref, vtd_ref):
    subs = [slice(k, k + SUB_TILE) for k in range(0, x_ref.shape[0], SUB_TILE)]
    zs = []
    for sl in subs:
        h = _rms(x_ref[sl, :], gpre_ref[0])
        h = h * (1.0 + mod_ref[0, 0, 1:2, :]) + mod_ref[0, 0, 0:1, :]
        zs.append(jnp.dot(h.astype(BF16), win_ref[0], preferred_element_type=F32))
    for sl, z in zip(subs, zs):
        _proj_groups(z, sl, cgq_ref, cwuq_ref, cgkv_ref, cwukv_ref, dgq_ref, dgk_ref,
                     cos_hd_ref, sin_hd_ref, cos_c_ref, sin_c_ref,
                     qta_ref, ka_ref, vta_ref, u_ref, qtc_ref, kc_ref, vtc_ref, qtd_ref, kd_ref, vtd_ref)


def _proj_groups(z, sl, cgq_ref, cwuq_ref, cgkv_ref, cwukv_ref, dgq_ref, dgk_ref,
                 cos_hd_ref, sin_hd_ref, cos_c_ref, sin_c_ref,
                 qta_ref, ka_ref, vta_ref, u_ref, qtc_ref, kc_ref, vtc_ref, qtd_ref, kd_ref, vtd_ref):
    r = z.shape[0]
    scale_hd = HEAD_DIM ** -0.5 * LOG2E
    scale_c = (C_NOPE + C_ROPE) ** -0.5 * LOG2E
    cos_hd, sin_hd = cos_hd_ref[:, sl], sin_hd_ref[:, sl]
    cos_c, sin_c = cos_c_ref[:, sl], sin_c_ref[:, sl]
    ones = _ones_rows(r)
    zeros64 = jnp.zeros((HEAD_DIM, r), BF16)

    def put_gqa_q(ref, qt, hd):
        g = hd // (N_HEADS // KV_HEADS)
        ref[hd, g * HEAD_DIM:(g + 1) * HEAD_DIM, sl] = (qt * scale_hd).astype(BF16)
        ref[hd, (1 - g) * HEAD_DIM:(2 - g) * HEAD_DIM, sl] = zeros64

    def put_v(ref, hd, vt):
        ref[hd, 0:HEAD_DIM, sl] = vt.astype(BF16)
        ref[hd, HEAD_DIM:V_ROWS, sl] = ones

    qa_t = z[:, 0:256].T
    for hd in range(N_HEADS):
        put_gqa_q(qta_ref, _rope_t(qa_t[hd * 64:(hd + 1) * 64], cos_hd, sin_hd), hd)
    ka_t = z[:, 256:384].T
    ka_t = jnp.concatenate([_rope_t(ka_t[g * 64:(g + 1) * 64], cos_hd, sin_hd) for g in range(KV_HEADS)], axis=0)
    ka_ref[0, sl, :] = ka_t.T.astype(BF16)
    va_t = z[:, 384:512].T
    for g in range(KV_HEADS):
        put_v(vta_ref, g, va_t[g * 64:(g + 1) * 64])

    u_ref[sl, :] = z[:, 512:768]

    cq = _rms(z[:, 768:1024], cgq_ref[0])
    qc_t = jnp.dot(cq.astype(BF16), cwuq_ref[0], preferred_element_type=F32).T
    ckv = _rms(z[:, 1024:1152], cgkv_ref[0])
    kvc_t = jnp.dot(ckv.astype(BF16), cwukv_ref[0], preferred_element_type=F32).T
    kr_t = _rope_t(z[:, 1152:1280].T[0:C_ROPE], cos_c, sin_c)
    pad32 = jnp.zeros((QK_PAD - C_NOPE - C_ROPE, r), F32)
    per_head_q = C_NOPE + C_ROPE
    for hd in range(N_HEADS):
        q_nope = qc_t[hd * per_head_q:hd * per_head_q + C_NOPE]
        q_rope = _rope_t(qc_t[hd * per_head_q + C_NOPE:(hd + 1) * per_head_q], cos_c, sin_c)
        qtc_ref[hd, :, sl] = (jnp.concatenate([q_nope, q_rope, pad32], axis=0) * scale_c).astype(BF16)
        k_nope = kvc_t[hd * 128:hd * 128 + C_NOPE]
        kc_ref[hd, sl, :] = jnp.concatenate([k_nope, kr_t, pad32], axis=0).T.astype(BF16)
        put_v(vtc_ref, hd, kvc_t[hd * 128 + C_NOPE:(hd + 1) * 128])

    def head_norm(xt, g_col):
        ms = jnp.mean(xt * xt, axis=0, keepdims=True)
        return xt * lax.rsqrt(ms + EPS) * g_col

    qd_t = z[:, 1280:1536].T
    for hd in range(N_HEADS):
        q = head_norm(qd_t[hd * 64:(hd + 1) * 64], dgq_ref[0])
        put_gqa_q(qtd_ref, _rope_t(q, cos_hd, sin_hd), hd)
    kd_t = z[:, 1536:1664].T
    kd_t = jnp.concatenate(
        [_rope_t(head_norm(kd_t[g * 64:(g + 1) * 64], dgk_ref[0]), cos_hd, sin_hd) for g in range(KV_HEADS)], axis=0)
    kd_ref[0, sl, :] = kd_t.T.astype(BF16)
    vd_t = z[:, 1664:1792].T
    for g in range(KV_HEADS):
        put_v(vtd_ref, g, vd_t[g * 64:(g + 1) * 64])


def _project(x, mod, stream, layer, p, rope):
    t = x.shape[0]
    r = min(ROW_TILE, t)
    cos_hd, sin_hd, cos_c, sin_c = rope
    lay3 = lambda i: (layer, 0, 0)
    tok = lambda i: (i, 0)
    feat3 = lambda i: (0, 0, i)
    tok3 = lambda i: (0, i, 0)
    out_shapes = (
        jax.ShapeDtypeStruct((N_HEADS, QK_PAD, t), BF16),
        jax.ShapeDtypeStruct((1, t, QK_PAD), BF16),
        jax.ShapeDtypeStruct((KV_HEADS, V_ROWS, t), BF16),
        jax.ShapeDtypeStruct((t, GROUP_WIDTH), F32),
        jax.ShapeDtypeStruct((N_HEADS, QK_PAD, t), BF16),
        jax.ShapeDtypeStruct((N_HEADS, t, QK_PAD), BF16),
        jax.ShapeDtypeStruct((N_HEADS, V_ROWS, t), BF16),
        jax.ShapeDtypeStruct((N_HEADS, QK_PAD, t), BF16),
        jax.ShapeDtypeStruct((1, t, QK_PAD), BF16),
        jax.ShapeDtypeStruct((KV_HEADS, V_ROWS, t), BF16),
    )
    out_specs = (
        pl.BlockSpec((N_HEADS, QK_PAD, r), feat3),
        pl.BlockSpec((1, r, QK_PAD), tok3),
        pl.BlockSpec((KV_HEADS, V_ROWS, r), feat3),
        pl.BlockSpec((r, GROUP_WIDTH), tok),
        pl.BlockSpec((N_HEADS, QK_PAD, r), feat3),
        pl.BlockSpec((N_HEADS, r, QK_PAD), tok3),
        pl.BlockSpec((N_HEADS, V_ROWS, r), feat3),
        pl.BlockSpec((N_HEADS, QK_PAD, r), feat3),
        pl.BlockSpec((1, r, QK_PAD), tok3),
        pl.BlockSpec((KV_HEADS, V_ROWS, r), feat3),
    )
    in_specs = [
        pl.BlockSpec((r, D_MODEL), tok),
        pl.BlockSpec((1, 1, N_MOD, D_MODEL), lambda i: (layer, stream, 0, 0)),
        pl.BlockSpec((1, 1, D_MODEL), lay3),
        pl.BlockSpec((1, D_MODEL, IN_PACKED), lay3),
        pl.BlockSpec((1, 1, C_Q_RANK), lay3),
        pl.BlockSpec((1, C_Q_RANK, p["c_wuq"].shape[2]), lay3),
        pl.BlockSpec((1, 1, C_KV_RANK), lay3),
        pl.BlockSpec((1, C_KV_RANK, p["c_wukv"].shape[2]), lay3),
        pl.BlockSpec((1, HEAD_DIM, 1), lay3),
        pl.BlockSpec((1, HEAD_DIM, 1), lay3),
        pl.BlockSpec((HEAD_DIM // 2, r), lambda i: (0, i)),
        pl.BlockSpec((HEAD_DIM // 2, r), lambda i: (0, i)),
        pl.BlockSpec((C_ROPE // 2, r), lambda i: (0, i)),
        pl.BlockSpec((C_ROPE // 2, r), lambda i: (0, i)),
    ]
    return pl.pallas_call(
        _proj_kernel,
        grid=(t // r,),
        in_specs=in_specs,
        out_specs=out_specs,
        out_shape=out_shapes,
        compiler_params=_cparams(("arbitrary",)),
        name="in_proj",
    )(x, mod, p["g_pre1"], p["w_in"], p["c_gq"], p["c_wuq"], p["c_gkv"], p["c_wukv"],
      p["d_gq"], p["d_gk"], cos_hd, sin_hd, cos_c, sin_c)


def _trip_blocks(n_blocks):
    return max(b for b in range(2, MAX_TRIP_BLOCKS + 1, 2) if (n_blocks - 2) % b == 0)


def _flash_tile(q, sink, kc_ref, vtc_ref, kl_ref, vtl_ref, m_ref, acc_ref, scratch, n_blocks, emit):
    s = jnp.dot(kc_ref[0], q, preferred_element_type=F32)

    if n_blocks:
        s_refs, cm_refs, p_refs, al_refs = scratch[0:2], scratch[2:4], scratch[4:6], scratch[6:8]
        parts = KV_BLOCK // MXU_DEPTH

        def produce_part(slot, j, h, cm):
            off = pl.multiple_of(j * KV_BLOCK + h * MXU_DEPTH, MXU_DEPTH)
            sj = jnp.dot(kl_ref[0, pl.ds(off, MXU_DEPTH), :], q, preferred_element_type=F32)
            s_refs[slot][h * MXU_DEPTH:(h + 1) * MXU_DEPTH, :] = sj
            cmh = jnp.max(sj, axis=0, keepdims=True)
            return cmh if cm is None else jnp.maximum(cm, cmh)

        def soften(slot):
            m_old = m_ref[...]
            m_new = jnp.maximum(m_old, cm_refs[slot][...])
            al_refs[slot][...] = jnp.exp2(m_old - m_new)
            p_refs[slot][...] = jnp.exp2(s_refs[slot][...] - m_new).astype(BF16)
            m_ref[...] = m_new

        def accumulate_part(slot, j, h):
            off = pl.multiple_of(j * KV_BLOCK + h * MXU_DEPTH, MXU_DEPTH)
            pv = jnp.dot(vtl_ref[0, :, pl.ds(off, MXU_DEPTH)], p_refs[slot][h * MXU_DEPTH:(h + 1) * MXU_DEPTH, :],
                         preferred_element_type=F32)
            scale = al_refs[slot][...] if h == 0 else 1.0
            acc_ref[...] = scale * acc_ref[...] + pv

        def stages(slot_p, j_p, slot_a, j_a):
            cm = None
            for h in range(parts):
                if j_p is not None:
                    cm = produce_part(slot_p, j_p, h, cm)
                if j_a is not None:
                    accumulate_part(slot_a, j_a, h)
            if j_p is not None:
                cm_refs[slot_p][...] = cm

        stages(0, 0, None, None)
    yield

    m = jnp.max(s, axis=0, keepdims=True)
    if sink is not None:
        m = jnp.maximum(m, sink)
    p = jnp.exp2(s - m)
    acc = jnp.dot(vtc_ref[0], p.astype(BF16), preferred_element_type=F32)
    if sink is not None:
        row = lax.broadcasted_iota(jnp.int32, acc.shape, 0)
        acc = acc + jnp.where(row == HEAD_DIM, jnp.exp2(sink - m), 0.0)

    if n_blocks:
        m_ref[...] = m
        acc_ref[...] = acc
        p_refs[1][...] = jnp.zeros(p_refs[1].shape, BF16)
        al_refs[1][...] = jnp.ones(al_refs[1].shape, F32)
        per_trip = _trip_blocks(n_blocks)
        stages(1, 1, None, None)
    yield

    def trip(t):
        for b in range(per_trip):
            j = t * per_trip + b
            soften(b % 2)
            stages(b % 2, j + 2, 1 - b % 2, jnp.maximum(j - 1, 0))

    yield (trip, (n_blocks - 2) // per_trip) if n_blocks else None

    for j in (n_blocks - 2, n_blocks - 1):
        if n_blocks:
            soften(j % 2)
            stages(None, None, 1 - j % 2, j - 1)
        yield

    if n_blocks:
        stages(None, None, (n_blocks - 1) % 2, n_blocks - 1)
        acc = acc_ref[...]
    emit(acc)
    yield


def _flash_kernel(*refs, n_blocks, has_sink, n_sub):
    refs = list(refs)
    sink = refs.pop(0)[pl.program_id(0)] if has_sink else None
    qt_ref, kc_ref, vtc_ref = refs[:3]
    kl_ref, vtl_ref = refs[3:5] if n_blocks else (None, None)
    refs = refs[5:] if n_blocks else refs[3:]
    o_ref, scratch = refs[0], refs[1:]
    per_sub = len(scratch) // n_sub
    tq = o_ref.shape[1] // n_sub

    def tile(k):
        def emit(acc):
            o_ref[:, k * tq:(k + 1) * tq] = (acc[0:HEAD_DIM] / acc[HEAD_DIM:HEAD_DIM + 1]).astype(o_ref.dtype)

        sub = scratch[k * per_sub:(k + 1) * per_sub]
        return _flash_tile(qt_ref[0, :, k * tq:(k + 1) * tq], sink, kc_ref, vtc_ref, kl_ref, vtl_ref,
                           sub[0], sub[1], sub[2:], n_blocks, emit)

    tiles = [tile(k) for k in range(n_sub)]
    groups = [tiles[k:k + FLASH_GROUP] for k in range(0, n_sub, FLASH_GROUP)]

    def advance(group):
        return [next(t) for t in group]

    def run_loop(group):
        loops = advance(group)
        if loops[0] is not None:
            def body(t, carry):
                for trip, _ in loops:
                    trip(t)
                return carry
            lax.fori_loop(0, loops[0][1], body, 0)

    advance(groups[0])
    advance(groups[0])
    run_loop(groups[0])
    for k, group in enumerate(groups):
        follower = groups[k + 1] if k + 1 < len(groups) else None
        for _ in range(2):
            if follower is not None:
                advance(follower)
            advance(group)
        advance(group)
        if follower is not None:
            run_loop(follower)


def _dense_attention(qt, k_ctx, vt_ctx, k_lat=None, vt_lat=None, sink=None, name="dense_attn"):
    n_heads, _, tq_total = qt.shape
    tq = min(Q_TILE, tq_total)
    n_sub = min(FLASH_SUB_TILES, tq_total // tq)
    step = tq * n_sub
    n_ctx = k_ctx.shape[1]
    k_rep = n_heads // k_ctx.shape[0]
    v_rep = n_heads // vt_ctx.shape[0]
    has_sink = sink is not None
    has_lat = k_lat is not None
    n_lat = k_lat.shape[1] if has_lat else 0
    n_blocks = n_lat // KV_BLOCK
    assert n_lat % (2 * KV_BLOCK) == 0 and n_blocks != 2
    args, in_specs = [], []
    if has_sink:
        args.append(sink)
        in_specs.append(pl.BlockSpec(memory_space=pltpu.SMEM))
    args += [qt, k_ctx, vt_ctx]
    in_specs += [
        pl.BlockSpec((1, QK_PAD, step), lambda h, i: (h, 0, i)),
        pl.BlockSpec((1, n_ctx, QK_PAD), lambda h, i: (h // k_rep, 0, 0)),
        pl.BlockSpec((1, V_ROWS, n_ctx), lambda h, i: (h // v_rep, 0, 0)),
    ]
    scratch = [pltpu.VMEM((1, tq), F32), pltpu.VMEM((V_ROWS, tq), F32)]
    if has_lat:
        args += [k_lat, vt_lat]
        in_specs += [
            pl.BlockSpec((1, n_lat, QK_PAD), lambda h, i: (h // k_rep, 0, 0)),
            pl.BlockSpec((1, V_ROWS, n_lat), lambda h, i: (h // v_rep, 0, 0)),
        ]
        scratch += (2 * [pltpu.VMEM((KV_BLOCK, tq), F32)] + 2 * [pltpu.VMEM((1, tq), F32)]
                    + 2 * [pltpu.VMEM((KV_BLOCK, tq), BF16)] + 2 * [pltpu.VMEM((1, tq), F32)])
    return pl.pallas_call(
        functools.partial(_flash_kernel, n_blocks=n_blocks, has_sink=has_sink, n_sub=n_sub),
        grid=(n_heads, tq_total // step),
        in_specs=in_specs,
        out_specs=pl.BlockSpec((HEAD_DIM, step), lambda h, i: (h, i)),
        out_shape=jax.ShapeDtypeStruct((n_heads * HEAD_DIM, tq_total), BF16),
        scratch_shapes=scratch * n_sub,
        compiler_params=_cparams(("arbitrary", "arbitrary")),
        name=name,
    )(*args)


def _window_kernel(sink_ref, qt_ref, kc_ref, vtc_ref, kp_ref, km_ref, kn_ref, vp_ref, vm_ref, vn_ref,
                   bias_ref, o_ref):
    i = pl.program_id(0)
    n = pl.num_programs(0)
    n_heads, n_kv = qt_ref.shape[0], vtc_ref.shape[0]
    per_kv = n_heads // n_kv
    n_tiles = WIN_STEP // WIN_TILE
    band = WIN_TILE + 2 * HALO
    dot = functools.partial(jnp.dot, preferred_element_type=F32)
    k_all = jnp.concatenate([kp_ref[0], km_ref[0], kn_ref[0]], axis=0)
    v_all = [jnp.concatenate([vp_ref[g], vm_ref[g], vn_ref[g]], axis=1) for g in range(n_kv)]
    lane_head = lax.broadcasted_iota(jnp.int32, (1, n_heads * WIN_TILE), 1) // WIN_TILE
    sink = functools.reduce(lambda acc, hd: jnp.where(lane_head == hd, sink_ref[hd], acc), range(n_heads),
                            jnp.zeros((1, n_heads * WIN_TILE), F32))
    row = lax.broadcasted_iota(jnp.int32, (band, 1), 0)

    def scores(t):
        q = jnp.concatenate([qt_ref[hd, :, t * WIN_TILE:(t + 1) * WIN_TILE] for hd in range(n_heads)], axis=1)
        s_band = dot(k_all[t * WIN_TILE:t * WIN_TILE + band], q) + bias_ref[...]
        if t == 0:
            s_band = s_band + jnp.where((row < HALO) & (i == 0), NEG_INF, 0.0)
        if t == n_tiles - 1:
            s_band = s_band + jnp.where((row >= band - HALO) & (i == n - 1), NEG_INF, 0.0)
        return dot(kc_ref[0], q), s_band

    def finish(t, ss):
        s_ctx, s_band = ss
        m = jnp.maximum(jnp.max(s_ctx, axis=0, keepdims=True), jnp.max(s_band, axis=0, keepdims=True))
        m = jnp.maximum(m, sink)
        p_ctx = jnp.exp2(s_ctx - m).astype(BF16)
        p_band = jnp.exp2(s_band - m).astype(BF16)
        sink_term = jnp.exp2(sink - m)
        for g in range(n_kv):
            cols = slice(g * per_kv * WIN_TILE, (g + 1) * per_kv * WIN_TILE)
            acc = (dot(vtc_ref[g], p_ctx[:, cols])
                   + dot(v_all[g][:, t * WIN_TILE:t * WIN_TILE + band], p_band[:, cols]))
            out = acc[0:HEAD_DIM] / (acc[HEAD_DIM:HEAD_DIM + 1] + sink_term[:, cols])
            for k in range(per_kv):
                hd = g * per_kv + k
                o_ref[hd * HEAD_DIM:(hd + 1) * HEAD_DIM, t * WIN_TILE:(t + 1) * WIN_TILE] = (
                    out[:, k * WIN_TILE:(k + 1) * WIN_TILE].astype(o_ref.dtype))

    ss = scores(0)
    for t in range(n_tiles):
        nxt = scores(t + 1) if t + 1 < n_tiles else None
        finish(t, ss)
        ss = nxt


def _window_bias(n_heads):
    r = np.arange(WIN_TILE + 2 * HALO)[:, None] - HALO
    c = np.arange(WIN_TILE)[None, :]
    bias = np.where(np.abs(c - r) <= WINDOW, 0.0, NEG_INF).astype(np.float32)
    return np.tile(bias, (1, n_heads))


def _window_attention(qt, k_ctx, vt_ctx, k_lat, vt_lat, sink):
    n_heads, _, t = qt.shape
    n_ctx = k_ctx.shape[1]
    n_kv = vt_ctx.shape[0]
    per = WIN_STEP // HALO
    last = t // HALO - 1
    prev_blk = lambda i: jnp.maximum(i * per - 1, 0)
    next_blk = lambda i: jnp.minimum(i * per + per, last)
    in_specs = [
        pl.BlockSpec(memory_space=pltpu.SMEM),
        pl.BlockSpec((n_heads, QK_PAD, WIN_STEP), lambda i: (0, 0, i)),
        pl.BlockSpec((1, n_ctx, QK_PAD), lambda i: (0, 0, 0)),
        pl.BlockSpec((n_kv, V_ROWS, n_ctx), lambda i: (0, 0, 0)),
        pl.BlockSpec((1, HALO, QK_PAD), lambda i: (0, prev_blk(i), 0)),
        pl.BlockSpec((1, WIN_STEP, QK_PAD), lambda i: (0, i, 0)),
        pl.BlockSpec((1, HALO, QK_PAD), lambda i: (0, next_blk(i), 0)),
        pl.BlockSpec((n_kv, V_ROWS, HALO), lambda i: (0, 0, prev_blk(i))),
        pl.BlockSpec((n_kv, V_ROWS, WIN_STEP), lambda i: (0, 0, i)),
        pl.BlockSpec((n_kv, V_ROWS, HALO), lambda i: (0, 0, next_blk(i))),
        pl.BlockSpec((WIN_TILE + 2 * HALO, n_heads * WIN_TILE), lambda i: (0, 0)),
    ]
    return pl.pallas_call(
        _window_kernel,
        grid=(t // WIN_STEP,),
        in_specs=in_specs,
        out_specs=pl.BlockSpec((n_heads * HEAD_DIM, WIN_STEP), lambda i: (0, i)),
        out_shape=jax.ShapeDtypeStruct((n_heads * HEAD_DIM, t), BF16),
        compiler_params=_cparams(("arbitrary",)),
        name="window_attn",
    )(sink, qt, k_ctx, vt_ctx, k_lat, k_lat, k_lat, vt_lat, vt_lat, vt_lat, jnp.asarray(_window_bias(n_heads)))


def _pool_kernel(u_ref, up_ref, un_ref, w_ref, scale_ref, o_ref, *, t_total):
    i = pl.program_id(0)
    n = pl.num_programs(0)
    r = u_ref.shape[0]
    u = u_ref[...]
    prev = jnp.where(i > 0, up_ref[...], 0.0)
    nxt = jnp.where(i < n - 1, un_ref[...], 0.0)
    ext = jnp.concatenate([prev, u, nxt], axis=0)
    rows = ext.shape[0]

    def shifted(x, k):
        return pltpu.roll(x, k % rows, axis=0)

    def window_sums(e):
        s2 = shifted(e, 1) + e
        s4 = shifted(s2, 1) + shifted(s2, -1)
        s8 = shifted(s4, 2) + shifted(s4, -2)
        s16 = shifted(s8, 4) + shifted(s8, -4)
        return s2, s4, s8, s16

    low_group = lax.broadcasted_iota(jnp.int32, (rows, LANE), 1) < POOL_GROUP
    s2, s4, _, _ = window_sums(ext[:, 0:LANE])
    _, _, s8, s16 = window_sums(ext[:, LANE:2 * LANE])
    total = jnp.concatenate([jnp.where(low_group, s2, s4), jnp.where(low_group, s8, s16)], axis=1)
    total = total[POOL_HALO:POOL_HALO + r]

    lane = lax.broadcasted_iota(jnp.int32, (r, GROUP_WIDTH), 1)
    pos = lax.broadcasted_iota(jnp.int32, (r, GROUP_WIDTH), 0) + i * r
    half = jnp.zeros((r, GROUP_WIDTH), jnp.int32)
    for gi, size in enumerate(POOL_SIZES):
        half = jnp.where((lane >= gi * POOL_GROUP) & (lane < (gi + 1) * POOL_GROUP), size // 2, half)
    count = jnp.minimum(pos + half, t_total) - jnp.maximum(pos - half, 0)
    y = total / count.astype(F32) - u
    o_ref[...] = (jnp.dot(y.astype(BF16), w_ref[0], preferred_element_type=F32) * scale_ref[0]).astype(o_ref.dtype)


def _pool(u, layer, p):
    t = u.shape[0]
    r = min(ROW_TILE, t)
    per = r // POOL_HALO
    last = t // POOL_HALO - 1
    return pl.pallas_call(
        functools.partial(_pool_kernel, t_total=t),
        grid=(t // r,),
        in_specs=[
            pl.BlockSpec((r, GROUP_WIDTH), lambda i: (i, 0)),
            pl.BlockSpec((POOL_HALO, GROUP_WIDTH), lambda i: (jnp.maximum(i * per - 1, 0), 0)),
            pl.BlockSpec((POOL_HALO, GROUP_WIDTH), lambda i: (jnp.minimum(i * per + per, last), 0)),
            pl.BlockSpec((1, GROUP_WIDTH, GROUP_WIDTH), lambda i: (layer, 0, 0)),
            pl.BlockSpec((1, 1, GROUP_WIDTH), lambda i: (layer, 0, 0)),
        ],
        out_specs=pl.BlockSpec((r, GROUP_WIDTH), lambda i: (i, 0)),
        out_shape=jax.ShapeDtypeStruct((t, GROUP_WIDTH), BF16),
        compiler_params=_cparams(("arbitrary",)),
        name="pool",
    )(u, u, u, p["pool_bd"], p["pool_scale"])


def _mix_ffn_kernel(x_ref, ota_ref, otc_ref, otd_ref, yb_ref, wout_ref, mod_ref, gpost1_ref,
                    gpre2_ref, wgu_ref, wdown_ref, gpost2_ref, o_ref):
    subs = [slice(k, k + SUB_TILE) for k in range(0, x_ref.shape[0], SUB_TILE)]
    n_att = ota_ref.shape[0] + otc_ref.shape[0] + otd_ref.shape[0]
    ys = []
    for sl in subs:
        ot = jnp.concatenate([ota_ref[:, sl], otc_ref[:, sl], otd_ref[:, sl]], axis=0)
        y = lax.dot_general(ot, wout_ref[0, 0:n_att, :], (((0,), (0,)), ((), ())), preferred_element_type=F32)
        ys.append(y + jnp.dot(yb_ref[sl, :], wout_ref[0, n_att:, :], preferred_element_type=F32))
    xs = [x_ref[sl, :] + mod_ref[0, 0, 2:3, :] * _rms(y, gpost1_ref[0]) for sl, y in zip(subs, ys)]
    gus = []
    for x in xs:
        h = _rms(x, gpre2_ref[0])
        h = h * (1.0 + mod_ref[0, 0, 4:5, :]) + mod_ref[0, 0, 3:4, :]
        gus.append(jnp.dot(h.astype(BF16), wgu_ref[0], preferred_element_type=F32))
    y2s = []
    for gu in gus:
        gate, up = gu[:, :D_FF], gu[:, D_FF:]
        act = (gate * jax.nn.sigmoid(gate) * up).astype(BF16)
        y2s.append(jnp.dot(act, wdown_ref[0], preferred_element_type=F32))
    for sl, x, y2 in zip(subs, xs, y2s):
        o_ref[sl, :] = x + mod_ref[0, 0, 5:6, :] * _rms(y2, gpost2_ref[0])


def _mix_ffn(x, ot_a, ot_c, ot_d, yb, mod, stream, layer, p):
    t = x.shape[0]
    r = min(ROW_TILE, t)
    feat = pl.BlockSpec((GROUP_WIDTH, r), lambda i: (0, i))
    lay3 = lambda i: (layer, 0, 0)
    once = pl.Buffered(1)
    return pl.pallas_call(
        _mix_ffn_kernel,
        grid=(t // r,),
        in_specs=[
            pl.BlockSpec((r, D_MODEL), lambda i: (i, 0)),
            feat, feat, feat,
            pl.BlockSpec((r, GROUP_WIDTH), lambda i: (i, 0)),
            pl.BlockSpec((1, D_MODEL, D_MODEL), lay3, pipeline_mode=once),
            pl.BlockSpec((1, 1, N_MOD, D_MODEL), lambda i: (layer, stream, 0, 0)),
            pl.BlockSpec((1, 1, D_MODEL), lay3),
            pl.BlockSpec((1, 1, D_MODEL), lay3),
            pl.BlockSpec((1, D_MODEL, 2 * D_FF), lay3, pipeline_mode=once),
            pl.BlockSpec((1, D_FF, D_MODEL), lay3, pipeline_mode=once),
            pl.BlockSpec((1, 1, D_MODEL), lay3),
        ],
        out_specs=pl.BlockSpec((r, D_MODEL), lambda i: (i, 0)),
        out_shape=jax.ShapeDtypeStruct((t, D_MODEL), F32),
        compiler_params=_cparams(("arbitrary",)),
        name="mix_ffn",
    )(x, ot_a, ot_c, ot_d, yb, p["w_out"], mod, p["g_post1"], p["g_pre2"], p["w_gu"], p["w_down"], p["g_post2"])


def _rope_tables(n):
    rows = n // GRID_W
    row = jnp.repeat(jnp.arange(rows, dtype=F32), GRID_W)
    col = jnp.tile(jnp.arange(GRID_W, dtype=F32), rows)
    out = []
    for rot_dim in (HEAD_DIM, C_ROPE):
        n_axis = rot_dim // 4
        freqs = ROPE_THETA ** (-jnp.arange(n_axis, dtype=F32) / n_axis)
        ang = jnp.concatenate([row[:, None] * freqs, col[:, None] * freqs], axis=-1)
        out += [jnp.cos(ang).T, jnp.sin(ang).T]
    return tuple(out)


def _identity_rope(n):
    return (jnp.ones((HEAD_DIM // 2, n), F32), jnp.zeros((HEAD_DIM // 2, n), F32),
            jnp.ones((C_ROPE // 2, n), F32), jnp.zeros((C_ROPE // 2, n), F32))


def _pack_params(w_in, pool_w, w_out, w_gu, w_down, c_wuq, c_wukv):
    depth = w_in.shape[0]
    pad = jnp.zeros((depth, D_MODEL, IN_PACKED - IN_WIDTH), w_in.dtype)
    w_in_p = jnp.concatenate([w_in[:, :, :KR_END], pad, w_in[:, :, KR_END:]], axis=2).astype(BF16)
    eye = jnp.eye(len(POOL_SIZES), dtype=pool_w.dtype)
    pool_bd = jnp.einsum("lgce,gh->lgche", pool_w, eye).reshape(depth, GROUP_WIDTH, GROUP_WIDTH).astype(BF16)
    w_out_p = jnp.concatenate([w_out[:, 0:256], w_out[:, 512:1024], w_out[:, 256:512]], axis=1).astype(BF16)
    return dict(w_in=w_in_p, pool_bd=pool_bd, w_out=w_out_p, w_gu=w_gu.astype(BF16), w_down=w_down.astype(BF16),
                c_wuq=c_wuq.astype(BF16), c_wukv=c_wukv.astype(BF16))


def kernel(x, c, ctx, c_ctx, w_ada, b_ada, g_pre1, g_post1, w_in, a_sink, pool_w, pool_scale, c_gq, c_wuq, c_gkv,
           c_wukv, d_gq, d_gk, w_out, g_pre2, g_post2, w_gu, w_down):
    assert x.shape[0] == 1 and ctx.shape[0] == 1
    depth = w_in.shape[0]
    n = x.shape[1]
    n_ctx = ctx.shape[1]
    p = _pack_params(w_in, pool_w, w_out, w_gu, w_down, c_wuq, c_wukv)
    row3 = lambda a: a.reshape(depth, 1, a.shape[1])
    p.update(g_pre1=row3(g_pre1), g_post1=row3(g_post1), g_pre2=row3(g_pre2), g_post2=row3(g_post2),
             pool_scale=row3(pool_scale), c_gq=row3(c_gq), c_gkv=row3(c_gkv),
             d_gq=d_gq.reshape(depth, HEAD_DIM, 1), d_gk=d_gk.reshape(depth, HEAD_DIM, 1))

    cond = jnp.zeros((8, D_MODEL), F32).at[0].set(c[0]).at[1].set(c_ctx)
    mod = _modulation(cond, w_ada, b_ada).reshape(depth, 8, N_MOD, D_MODEL)

    rope_x = _rope_tables(n)
    rope_c = _identity_rope(n_ctx)
    xs, cs = x[0], ctx[0]
    for layer in range(depth):
        last = layer == depth - 1
        sink = a_sink[layer] * LOG2E
        qta, ka, vta, ub, qtc, kc, vtc, qtd, kd, vtd = _project(xs, mod, 0, layer, p, rope_x)
        qta_c, ka_c, vta_c, ub_c, qtc_c, kc_c, vtc_c, qtd_c, kd_c, vtd_c = _project(cs, mod, 1, layer, p, rope_c)

        ot_a = _window_attention(qta, ka_c, vta_c, ka, vta, sink)
        yb = _pool(ub, layer, p)
        ot_c = _dense_attention(qtc, kc_c, vtc_c, kc, vtc, name="dense_attn_c")
        ot_d = _dense_attention(qtd, kd_c, vtd_c, kd, vtd, name="dense_attn_d")
        xs = _mix_ffn(xs, ot_a, ot_c, ot_d, yb, mod, 0, layer, p)

        if not last:
            ot_a_c = _dense_attention(qta_c, ka_c, vta_c, sink=sink, name="ctx_attn_a")
            yb_c = _pool(ub_c, layer, p)
            ot_c_c = _dense_attention(qtc_c, kc_c, vtc_c, name="ctx_attn_c")
            ot_d_c = _dense_attention(qtd_c, kd_c, vtd_c, name="ctx_attn_d")
            cs = _mix_ffn(cs, ot_a_c, ot_c_c, ot_d_c, yb_c, mod, 1, layer, p)
    return xs[None]
```

```python
import functools

import numpy as np
import jax
import jax.numpy as jnp
from jax import lax
from jax.experimental import pallas as pl
from jax.experimental.pallas import tpu as pltpu

F32 = jnp.float32
BF16 = jnp.bfloat16

D_MODEL = 1024
GRID_W = 64
GROUP_WIDTH = 256
HEAD_DIM = 64
ROPE_THETA = 10000.0
EPS = 1e-6
NEG_INF = -1e30
LOG2E = 1.4426950408889634
N_MOD = 6
N_HEADS = 4
KV_HEADS = 2
WINDOW = 128
POOL_SIZES = (2, 4, 8, 16)
POOL_GROUP = 64
C_NOPE, C_ROPE, C_V = 64, 32, 64
C_Q_RANK, C_KV_RANK = 256, 128
D_FF = 2816
IN_WIDTH = 1696
KR_END = 1184
IN_PACKED = 1792

LANE = 128
QK_PAD = 128
V_ROWS = 80
ROW_TILE = 512
SUB_TILE = 256
Q_TILE = 512
WIN_TILE = 256
WIN_STEP = 1024
FLASH_GROUP = 2
FLASH_SUB_TILES = 4
KV_BLOCK = 1024
MXU_DEPTH = 256
MAX_TRIP_BLOCKS = 6
HALO = 128
POOL_HALO = 8
VMEM_LIMIT = 56 * 1024 * 1024


def _cparams(sem):
    return pltpu.CompilerParams(dimension_semantics=sem, vmem_limit_bytes=VMEM_LIMIT)


def _rms(x, g):
    ms = jnp.mean(x * x, axis=-1, keepdims=True)
    return x * lax.rsqrt(ms + EPS) * g


def _split_bf16(x):
    hi = x.astype(BF16)
    lo = (x - hi.astype(F32)).astype(BF16)
    return hi, lo


def _mod_kernel(cond_ref, w_ref, b_ref, o_ref):
    c = cond_ref[...]
    s = c * jax.nn.sigmoid(c)
    s_hi, s_lo = _split_bf16(s)
    w_hi, w_lo = _split_bf16(w_ref[0])
    acc = jnp.dot(s_hi, w_hi, preferred_element_type=F32)
    acc += jnp.dot(s_hi, w_lo, preferred_element_type=F32)
    acc += jnp.dot(s_lo, w_hi, preferred_element_type=F32)
    o_ref[0] = acc + b_ref[0]


def _modulation(cond, w_ada, b_ada):
    depth, d, n = w_ada.shape
    tn = 1536
    return pl.pallas_call(
        _mod_kernel,
        grid=(depth, n // tn),
        in_specs=[
            pl.BlockSpec((8, d), lambda l, j: (0, 0)),
            pl.BlockSpec((1, d, tn), lambda l, j: (l, 0, j)),
            pl.BlockSpec((1, 1, tn), lambda l, j: (l, 0, j)),
        ],
        out_specs=pl.BlockSpec((1, 8, tn), lambda l, j: (l, 0, j)),
        out_shape=jax.ShapeDtypeStruct((depth, 8, n), F32),
        compiler_params=_cparams(("arbitrary", "arbitrary")),
        name="modulation",
    )(cond, w_ada, b_ada.reshape(depth, 1, n))


def _rope_t(xt, cos, sin):
    half = xt.shape[0] // 2
    x1, x2 = xt[:half], xt[half:]
    return jnp.concatenate([x1 * cos - x2 * sin, x2 * cos + x1 * sin], axis=0)


def _ones_rows(r):
    row = lax.broadcasted_iota(jnp.int32, (V_ROWS - HEAD_DIM, r), 0)
    return jnp.where(row == 0, 1.0, 0.0).astype(BF16)


def _proj_kernel(x_ref, mod_ref, gpre_ref, win_ref, cgq_ref, cwuq_ref, cgkv_ref, cwukv_ref,
                 dgq_ref, dgk_ref, cos_hd_ref, sin_hd_ref, cos_c_ref, sin_c_ref,
                 qta_ref, ka_ref, vta_ref, u_ref, qtc_ref, kc_ref, vtc_ref, qtd_ref, kd_ref, vtd_ref):
    subs = [slice(k, k + SUB_TILE) for k in range(0, x_ref.shape[0], SUB_TILE)]
    zs = []
    for sl in subs:
        h = _rms(x_ref[sl, :], gpre_ref[0])
        h = h * (1.0 + mod_ref[0, 0, 1:2, :]) + mod_ref[0, 0, 0:1, :]
        zs.append(jnp.dot(h.astype(BF16), win_ref[0], preferred_element_type=F32))
    for sl, z in zip(subs, zs):
        _proj_groups(z, sl, cgq_ref, cwuq_ref, cgkv_ref, cwukv_ref, dgq_ref, dgk_ref,
                     cos_hd_ref, sin_hd_ref, cos_c_ref, sin_c_ref,
                     qta_ref, ka_ref, vta_ref, u_ref, qtc_ref, kc_ref, vtc_ref, qtd_ref, kd_ref, vtd_ref)


def _proj_groups(z, sl, cgq_ref, cwuq_ref, cgkv_ref, cwukv_ref, dgq_ref, dgk_ref,
                 cos_hd_ref, sin_hd_ref, cos_c_ref, sin_c_ref,
                 qta_ref, ka_ref, vta_ref, u_ref, qtc_ref, kc_ref, vtc_ref, qtd_ref, kd_ref, vtd_ref):
    r = z.shape[0]
    scale_hd = HEAD_DIM ** -0.5 * LOG2E
    scale_c = (C_NOPE + C_ROPE) ** -0.5 * LOG2E
    cos_hd, sin_hd = cos_hd_ref[:, sl], sin_hd_ref[:, sl]
    cos_c, sin_c = cos_c_ref[:, sl], sin_c_ref[:, sl]
    ones = _ones_rows(r)
    zeros64 = jnp.zeros((HEAD_DIM, r), BF16)

    def put_gqa_q(ref, qt, hd):
        g = hd // (N_HEADS // KV_HEADS)
        ref[hd, g * HEAD_DIM:(g + 1) * HEAD_DIM, sl] = (qt * scale_hd).astype(BF16)
        ref[hd, (1 - g) * HEAD_DIM:(2 - g) * HEAD_DIM, sl] = zeros64

    def put_v(ref, hd, vt):
        ref[hd, 0:HEAD_DIM, sl] = vt.astype(BF16)
        ref[hd, HEAD_DIM:V_ROWS, sl] = ones

    qa_t = z[:, 0:256].T
    for hd in range(N_HEADS):
        put_gqa_q(qta_ref, _rope_t(qa_t[hd * 64:(hd + 1) * 64], cos_hd, sin_hd), hd)
    ka_t = z[:, 256:384].T
    ka_t = jnp.concatenate([_rope_t(ka_t[g * 64:(g + 1) * 64], cos_hd, sin_hd) for g in range(KV_HEADS)], axis=0)
    ka_ref[0, sl, :] = ka_t.T.astype(BF16)
    va_t = z[:, 384:512].T
    for g in range(KV_HEADS):
        put_v(vta_ref, g, va_t[g * 64:(g + 1) * 64])

    u_ref[sl, :] = z[:, 512:768]

    cq = _rms(z[:, 768:1024], cgq_ref[0])
    qc_t = jnp.dot(cq.astype(BF16), cwuq_ref[0], preferred_element_type=F32).T
    ckv = _rms(z[:, 1024:1152], cgkv_ref[0])
    kvc_t = jnp.dot(ckv.astype(BF16), cwukv_ref[0], preferred_element_type=F32).T
    kr_t = _rope_t(z[:, 1152:1280].T[0:C_ROPE], cos_c, sin_c)
    pad32 = jnp.zeros((QK_PAD - C_NOPE - C_ROPE, r), F32)
    per_head_q = C_NOPE + C_ROPE
    for hd in range(N_HEADS):
        q_nope = qc_t[hd * per_head_q:hd * per_head_q + C_NOPE]
        q_rope = _rope_t(qc_t[hd * per_head_q + C_NOPE:(hd + 1) * per_head_q], cos_c, sin_c)
        qtc_ref[hd, :, sl] = (jnp.concatenate([q_nope, q_rope, pad32], axis=0) * scale_c).astype(BF16)
        k_nope = kvc_t[hd * 128:hd * 128 + C_NOPE]
        kc_ref[hd, sl, :] = jnp.concatenate([k_nope, kr_t, pad32], axis=0).T.astype(BF16)
        put_v(vtc_ref, hd, kvc_t[hd * 128 + C_NOPE:(hd + 1) * 128])

    def head_norm(xt, g_col):
        ms = jnp.mean(xt * xt, axis=0, keepdims=True)
        return xt * lax.rsqrt(ms + EPS) * g_col

    qd_t = z[:, 1280:1536].T
    for hd in range(N_HEADS):
        q = head_norm(qd_t[hd * 64:(hd + 1) * 64], dgq_ref[0])
        put_gqa_q(qtd_ref, _rope_t(q, cos_hd, sin_hd), hd)
    kd_t = z[:, 1536:1664].T
    kd_t = jnp.concatenate(
        [_rope_t(head_norm(kd_t[g * 64:(g + 1) * 64], dgk_ref[0]), cos_hd, sin_hd) for g in range(KV_HEADS)], axis=0)
    kd_ref[0, sl, :] = kd_t.T.astype(BF16)
    vd_t = z[:, 1664:1792].T
    for g in range(KV_HEADS):
        put_v(vtd_ref, g, vd_t[g * 64:(g + 1) * 64])


def _project(x, mod, stream, layer, p, rope):
    t = x.shape[0]
    r = min(ROW_TILE, t)
    cos_hd, sin_hd, cos_c, sin_c = rope
    lay3 = lambda i: (layer, 0, 0)
    tok = lambda i: (i, 0)
    feat3 = lambda i: (0, 0, i)
    tok3 = lambda i: (0, i, 0)
    out_shapes = (
        jax.ShapeDtypeStruct((N_HEADS, QK_PAD, t), BF16),
        jax.ShapeDtypeStruct((1, t, QK_PAD), BF16),
        jax.ShapeDtypeStruct((KV_HEADS, V_ROWS, t), BF16),
        jax.ShapeDtypeStruct((t, GROUP_WIDTH), F32),
        jax.ShapeDtypeStruct((N_HEADS, QK_PAD, t), BF16),
        jax.ShapeDtypeStruct((N_HEADS, t, QK_PAD), BF16),
        jax.ShapeDtypeStruct((N_HEADS, V_ROWS, t), BF16),
        jax.ShapeDtypeStruct((N_HEADS, QK_PAD, t), BF16),
        jax.ShapeDtypeStruct((1, t, QK_PAD), BF16),
        jax.ShapeDtypeStruct((KV_HEADS, V_ROWS, t), BF16),
    )
    out_specs = (
        pl.BlockSpec((N_HEADS, QK_PAD, r), feat3),
        pl.BlockSpec((1, r, QK_PAD), tok3),
        pl.BlockSpec((KV_HEADS, V_ROWS, r), feat3),
        pl.BlockSpec((r, GROUP_WIDTH), tok),
        pl.BlockSpec((N_HEADS, QK_PAD, r), feat3),
        pl.BlockSpec((N_HEADS, r, QK_PAD), tok3),
        pl.BlockSpec((N_HEADS, V_ROWS, r), feat3),
        pl.BlockSpec((N_HEADS, QK_PAD, r), feat3),
        pl.BlockSpec((1, r, QK_PAD), tok3),
        pl.BlockSpec((KV_HEADS, V_ROWS, r), feat3),
    )
    in_specs = [
        pl.BlockSpec((r, D_MODEL), tok),
        pl.BlockSpec((1, 1, N_MOD, D_MODEL), lambda i: (layer, stream, 0, 0)),
        pl.BlockSpec((1, 1, D_MODEL), lay3),
        pl.BlockSpec((1, D_MODEL, IN_PACKED), lay3),
        pl.BlockSpec((1, 1, C_Q_RANK), lay3),
        pl.BlockSpec((1, C_Q_RANK, p["c_wuq"].shape[2]), lay3),
        pl.BlockSpec((1, 1, C_KV_RANK), lay3),
        pl.BlockSpec((1, C_KV_RANK, p["c_wukv"].shape[2]), lay3),
        pl.BlockSpec((1, HEAD_DIM, 1), lay3),
        pl.BlockSpec((1, HEAD_DIM, 1), lay3),
        pl.BlockSpec((HEAD_DIM // 2, r), lambda i: (0, i)),
        pl.BlockSpec((HEAD_DIM // 2, r), lambda i: (0, i)),
        pl.BlockSpec((C_ROPE // 2, r), lambda i: (0, i)),
        pl.BlockSpec((C_ROPE // 2, r), lambda i: (0, i)),
    ]
    return pl.pallas_call(
        _proj_kernel,
        grid=(t // r,),
        in_specs=in_specs,
        out_specs=out_specs,
        out_shape=out_shapes,
        compiler_params=_cparams(("arbitrary",)),
        name="in_proj",
    )(x, mod, p["g_pre1"], p["w_in"], p["c_gq"], p["c_wuq"], p["c_gkv"], p["c_wukv"],
      p["d_gq"], p["d_gk"], cos_hd, sin_hd, cos_c, sin_c)


def _trip_blocks(n_blocks):
    return max(b for b in range(2, MAX_TRIP_BLOCKS + 1, 2) if (n_blocks - 2) % b == 0)


def _flash_tile(q, sink, kc_ref, vtc_ref, kl_ref, vtl_ref, m_ref, acc_ref, scratch, n_blocks, emit):
    s = jnp.dot(kc_ref[0], q, preferred_element_type=F32)

    if n_blocks:
        s_refs, p_refs, al_refs = scratch[0:2], scratch[2:4], scratch[4:6]
        parts = KV_BLOCK // MXU_DEPTH

        def produce_part(slot, j, h):
            off = pl.multiple_of(j * KV_BLOCK + h * MXU_DEPTH, MXU_DEPTH)
            sj = jnp.dot(kl_ref[0, pl.ds(off, MXU_DEPTH), :], q, preferred_element_type=F32)
            s_refs[slot][h * MXU_DEPTH:(h + 1) * MXU_DEPTH, :] = sj

        def soften(slot):
            sj = s_refs[slot][...]
            m_old = m_ref[...]
            m_new = jnp.maximum(m_old, jnp.max(sj, axis=0, keepdims=True))
            al_refs[slot][...] = jnp.exp2(m_old - m_new)
            p_refs[slot][...] = jnp.exp2(sj - m_new).astype(BF16)
            m_ref[...] = m_new

        def accumulate_part(slot, j, h):
            off = pl.multiple_of(j * KV_BLOCK + h * MXU_DEPTH, MXU_DEPTH)
            pv = jnp.dot(vtl_ref[0, :, pl.ds(off, MXU_DEPTH)], p_refs[slot][h * MXU_DEPTH:(h + 1) * MXU_DEPTH, :],
                         preferred_element_type=F32)
            scale = al_refs[slot][...] if h == 0 else 1.0
            acc_ref[...] = scale * acc_ref[...] + pv

        def stages(slot_p, j_p, slot_a, j_a):
            for h in range(parts):
                if j_p is not None:
                    produce_part(slot_p, j_p, h)
                if j_a is not None:
                    accumulate_part(slot_a, j_a, h)

        stages(0, 0, None, None)
    yield

    m = jnp.max(s, axis=0, keepdims=True)
    if sink is not None:
        m = jnp.maximum(m, sink)
    p = jnp.exp2(s - m)
    acc = jnp.dot(vtc_ref[0], p.astype(BF16), preferred_element_type=F32)
    if sink is not None:
        row = lax.broadcasted_iota(jnp.int32, acc.shape, 0)
        acc = acc + jnp.where(row == HEAD_DIM, jnp.exp2(sink - m), 0.0)

    if n_blocks:
        m_ref[...] = m
        acc_ref[...] = acc
        p_refs[1][...] = jnp.zeros(p_refs[1].shape, BF16)
        al_refs[1][...] = jnp.ones(al_refs[1].shape, F32)
        per_trip = _trip_blocks(n_blocks)
        stages(1, 1, None, None)
    yield

    def trip(t):
        for b in range(per_trip):
            j = t * per_trip + b
            soften(b % 2)
            stages(b % 2, j + 2, 1 - b % 2, jnp.maximum(j - 1, 0))

    yield (trip, (n_blocks - 2) // per_trip) if n_blocks else None

    for j in (n_blocks - 2, n_blocks - 1):
        if n_blocks:
            soften(j % 2)
            stages(None, None, 1 - j % 2, j - 1)
        yield

    if n_blocks:
        stages(None, None, (n_blocks - 1) % 2, n_blocks - 1)
        acc = acc_ref[...]
    emit(acc)
    yield


def _flash_kernel(*refs, n_blocks, has_sink, n_sub):
    refs = list(refs)
    sink = refs.pop(0)[pl.program_id(0)] if has_sink else None
    qt_ref, kc_ref, vtc_ref = refs[:3]
    kl_ref, vtl_ref = refs[3:5] if n_blocks else (None, None)
    refs = refs[5:] if n_blocks else refs[3:]
    o_ref, scratch = refs[0], refs[1:]
    per_sub = len(scratch) // n_sub
    tq = o_ref.shape[1] // n_sub

    def tile(k):
        def emit(acc):
            o_ref[:, k * tq:(k + 1) * tq] = (acc[0:HEAD_DIM] / acc[HEAD_DIM:HEAD_DIM + 1]).astype(o_ref.dtype)

        sub = scratch[k * per_sub:(k + 1) * per_sub]
        return _flash_tile(qt_ref[0, :, k * tq:(k + 1) * tq], sink, kc_ref, vtc_ref, kl_ref, vtl_ref,
                           sub[0], sub[1], sub[2:], n_blocks, emit)

    tiles = [tile(k) for k in range(n_sub)]
    groups = [tiles[k:k + FLASH_GROUP] for k in range(0, n_sub, FLASH_GROUP)]

    def advance(group):
        return [next(t) for t in group]

    def run_loop(group):
        loops = advance(group)
        if loops[0] is not None:
            def body(t, carry):
                for trip, _ in loops:
                    trip(t)
                return carry
            lax.fori_loop(0, loops[0][1], body, 0)

    advance(groups[0])
    advance(groups[0])
    run_loop(groups[0])
    for k, group in enumerate(groups):
        follower = groups[k + 1] if k + 1 < len(groups) else None
        for _ in range(2):
            if follower is not None:
                advance(follower)
            advance(group)
        advance(group)
        if follower is not None:
            run_loop(follower)


def _dense_attention(qt, k_ctx, vt_ctx, k_lat=None, vt_lat=None, sink=None, name="dense_attn"):
    n_heads, _, tq_total = qt.shape
    tq = min(Q_TILE, tq_total)
    n_sub = min(FLASH_SUB_TILES, tq_total // tq)
    step = tq * n_sub
    n_ctx = k_ctx.shape[1]
    k_rep = n_heads // k_ctx.shape[0]
    v_rep = n_heads // vt_ctx.shape[0]
    has_sink = sink is not None
    has_lat = k_lat is not None
    n_lat = k_lat.shape[1] if has_lat else 0
    n_blocks = n_lat // KV_BLOCK
    assert n_lat % (2 * KV_BLOCK) == 0 and n_blocks != 2
    args, in_specs = [], []
    if has_sink:
        args.append(sink)
        in_specs.append(pl.BlockSpec(memory_space=pltpu.SMEM))
    args += [qt, k_ctx, vt_ctx]
    in_specs += [
        pl.BlockSpec((1, QK_PAD, step), lambda h, i: (h, 0, i)),
        pl.BlockSpec((1, n_ctx, QK_PAD), lambda h, i: (h // k_rep, 0, 0)),
        pl.BlockSpec((1, V_ROWS, n_ctx), lambda h, i: (h // v_rep, 0, 0)),
    ]
    scratch = [pltpu.VMEM((1, tq), F32), pltpu.VMEM((V_ROWS, tq), F32)]
    if has_lat:
        args += [k_lat, vt_lat]
        in_specs += [
            pl.BlockSpec((1, n_lat, QK_PAD), lambda h, i: (h // k_rep, 0, 0)),
            pl.BlockSpec((1, V_ROWS, n_lat), lambda h, i: (h // v_rep, 0, 0)),
        ]
        scratch += (2 * [pltpu.VMEM((KV_BLOCK, tq), F32)] + 2 * [pltpu.VMEM((KV_BLOCK, tq), BF16)]
                    + 2 * [pltpu.VMEM((1, tq), F32)])
    return pl.pallas_call(
        functools.partial(_flash_kernel, n_blocks=n_blocks, has_sink=has_sink, n_sub=n_sub),
        grid=(n_heads, tq_total // step),
        in_specs=in_specs,
        out_specs=pl.BlockSpec((HEAD_DIM, step), lambda h, i: (h, i)),
        out_shape=jax.ShapeDtypeStruct((n_heads * HEAD_DIM, tq_total), BF16),
        scratch_shapes=scratch * n_sub,
        compiler_params=_cparams(("arbitrary", "arbitrary")),
        name=name,
    )(*args)


def _window_kernel(sink_ref, qt_ref, kc_ref, vtc_ref, kp_ref, km_ref, kn_ref, vp_ref, vm_ref, vn_ref,
                   bias_ref, o_ref):
    i = pl.program_id(0)
    n = pl.num_programs(0)
    n_heads, n_kv = qt_ref.shape[0], vtc_ref.shape[0]
    per_kv = n_heads // n_kv
    n_tiles = WIN_STEP // WIN_TILE
    band = WIN_TILE + 2 * HALO
    dot = functools.partial(jnp.dot, preferred_element_type=F32)
    k_all = jnp.concatenate([kp_ref[0], km_ref[0], kn_ref[0]], axis=0)
    v_all = [jnp.concatenate([vp_ref[g], vm_ref[g], vn_ref[g]], axis=1) for g in range(n_kv)]
    lane_head = lax.broadcasted_iota(jnp.int32, (1, n_heads * WIN_TILE), 1) // WIN_TILE
    sink = functools.reduce(lambda acc, hd: jnp.where(lane_head == hd, sink_ref[hd], acc), range(n_heads),
                            jnp.zeros((1, n_heads * WIN_TILE), F32))
    row = lax.broadcasted_iota(jnp.int32, (band, 1), 0)

    def scores(t):
        q = jnp.concatenate([qt_ref[hd, :, t * WIN_TILE:(t + 1) * WIN_TILE] for hd in range(n_heads)], axis=1)
        s_band = dot(k_all[t * WIN_TILE:t * WIN_TILE + band], q) + bias_ref[...]
        if t == 0:
            s_band = s_band + jnp.where((row < HALO) & (i == 0), NEG_INF, 0.0)
        if t == n_tiles - 1:
            s_band = s_band + jnp.where((row >= band - HALO) & (i == n - 1), NEG_INF, 0.0)
        return dot(kc_ref[0], q), s_band

    def finish(t, ss):
        s_ctx, s_band = ss
        m = jnp.maximum(jnp.max(s_ctx, axis=0, keepdims=True), jnp.max(s_band, axis=0, keepdims=True))
        m = jnp.maximum(m, sink)
        p_ctx = jnp.exp2(s_ctx - m).astype(BF16)
        p_band = jnp.exp2(s_band - m).astype(BF16)
        sink_term = jnp.exp2(sink - m)
        for g in range(n_kv):
            cols = slice(g * per_kv * WIN_TILE, (g + 1) * per_kv * WIN_TILE)
            acc = (dot(vtc_ref[g], p_ctx[:, cols])
                   + dot(v_all[g][:, t * WIN_TILE:t * WIN_TILE + band], p_band[:, cols]))
            out = acc[0:HEAD_DIM] / (acc[HEAD_DIM:HEAD_DIM + 1] + sink_term[:, cols])
            for k in range(per_kv):
                hd = g * per_kv + k
                o_ref[hd * HEAD_DIM:(hd + 1) * HEAD_DIM, t * WIN_TILE:(t + 1) * WIN_TILE] = (
                    out[:, k * WIN_TILE:(k + 1) * WIN_TILE].astype(o_ref.dtype))

    ss = scores(0)
    for t in range(n_tiles):
        nxt = scores(t + 1) if t + 1 < n_tiles else None
        finish(t, ss)
        ss = nxt


def _window_bias(n_heads):
    r = np.arange(WIN_TILE + 2 * HALO)[:, None] - HALO
    c = np.arange(WIN_TILE)[None, :]
    bias = np.where(np.abs(c - r) <= WINDOW, 0.0, NEG_INF).astype(np.float32)
    return np.tile(bias, (1, n_heads))


def _window_attention(qt, k_ctx, vt_ctx, k_lat, vt_lat, sink):
    n_heads, _, t = qt.shape
    n_ctx = k_ctx.shape[1]
    n_kv = vt_ctx.shape[0]
    per = WIN_STEP // HALO
    last = t // HALO - 1
    prev_blk = lambda i: jnp.maximum(i * per - 1, 0)
    next_blk = lambda i: jnp.minimum(i * per + per, last)
    in_specs = [
        pl.BlockSpec(memory_space=pltpu.SMEM),
        pl.BlockSpec((n_heads, QK_PAD, WIN_STEP), lambda i: (0, 0, i)),
        pl.BlockSpec((1, n_ctx, QK_PAD), lambda i: (0, 0, 0)),
        pl.BlockSpec((n_kv, V_ROWS, n_ctx), lambda i: (0, 0, 0)),
        pl.BlockSpec((1, HALO, QK_PAD), lambda i: (0, prev_blk(i), 0)),
        pl.BlockSpec((1, WIN_STEP, QK_PAD), lambda i: (0, i, 0)),
        pl.BlockSpec((1, HALO, QK_PAD), lambda i: (0, next_blk(i), 0)),
        pl.BlockSpec((n_kv, V_ROWS, HALO), lambda i: (0, 0, prev_blk(i))),
        pl.BlockSpec((n_kv, V_ROWS, WIN_STEP), lambda i: (0, 0, i)),
        pl.BlockSpec((n_kv, V_ROWS, HALO), lambda i: (0, 0, next_blk(i))),
        pl.BlockSpec((WIN_TILE + 2 * HALO, n_heads * WIN_TILE), lambda i: (0, 0)),
    ]
    return pl.pallas_call(
        _window_kernel,
        grid=(t // WIN_STEP,),
        in_specs=in_specs,
        out_specs=pl.BlockSpec((n_heads * HEAD_DIM, WIN_STEP), lambda i: (0, i)),
        out_shape=jax.ShapeDtypeStruct((n_heads * HEAD_DIM, t), BF16),
        compiler_params=_cparams(("arbitrary",)),
        name="window_attn",
    )(sink, qt, k_ctx, vt_ctx, k_lat, k_lat, k_lat, vt_lat, vt_lat, vt_lat, jnp.asarray(_window_bias(n_heads)))


def _pool_kernel(u_ref, up_ref, un_ref, w_ref, scale_ref, o_ref, *, t_total):
    i = pl.program_id(0)
    n = pl.num_programs(0)
    r = u_ref.shape[0]
    u = u_ref[...]
    prev = jnp.where(i > 0, up_ref[...], 0.0)
    nxt = jnp.where(i < n - 1, un_ref[...], 0.0)
    ext = jnp.concatenate([prev, u, nxt], axis=0)
    rows = ext.shape[0]

    def shifted(x, k):
        return pltpu.roll(x, k % rows, axis=0)

    def window_sums(e):
        s2 = shifted(e, 1) + e
        s4 = shifted(s2, 1) + shifted(s2, -1)
        s8 = shifted(s4, 2) + shifted(s4, -2)
        s16 = shifted(s8, 4) + shifted(s8, -4)
        return s2, s4, s8, s16

    low_group = lax.broadcasted_iota(jnp.int32, (rows, LANE), 1) < POOL_GROUP
    s2, s4, _, _ = window_sums(ext[:, 0:LANE])
    _, _, s8, s16 = window_sums(ext[:, LANE:2 * LANE])
    total = jnp.concatenate([jnp.where(low_group, s2, s4), jnp.where(low_group, s8, s16)], axis=1)
    total = total[POOL_HALO:POOL_HALO + r]

    lane = lax.broadcasted_iota(jnp.int32, (r, GROUP_WIDTH), 1)
    pos = lax.broadcasted_iota(jnp.int32, (r, GROUP_WIDTH), 0) + i * r
    half = jnp.zeros((r, GROUP_WIDTH), jnp.int32)
    for gi, size in enumerate(POOL_SIZES):
        half = jnp.where((lane >= gi * POOL_GROUP) & (lane < (gi + 1) * POOL_GROUP), size // 2, half)
    count = jnp.minimum(pos + half, t_total) - jnp.maximum(pos - half, 0)
    y = total / count.astype(F32) - u
    o_ref[...] = (jnp.dot(y.astype(BF16), w_ref[0], preferred_element_type=F32) * scale_ref[0]).astype(o_ref.dtype)


def _pool(u, layer, p):
    t = u.shape[0]
    r = min(ROW_TILE, t)
    per = r // POOL_HALO
    last = t // POOL_HALO - 1
    return pl.pallas_call(
        functools.partial(_pool_kernel, t_total=t),
        grid=(t // r,),
        in_specs=[
            pl.BlockSpec((r, GROUP_WIDTH), lambda i: (i, 0)),
            pl.BlockSpec((POOL_HALO, GROUP_WIDTH), lambda i: (jnp.maximum(i * per - 1, 0), 0)),
            pl.BlockSpec((POOL_HALO, GROUP_WIDTH), lambda i: (jnp.minimum(i * per + per, last), 0)),
            pl.BlockSpec((1, GROUP_WIDTH, GROUP_WIDTH), lambda i: (layer, 0, 0)),
            pl.BlockSpec((1, 1, GROUP_WIDTH), lambda i: (layer, 0, 0)),
        ],
        out_specs=pl.BlockSpec((r, GROUP_WIDTH), lambda i: (i, 0)),
        out_shape=jax.ShapeDtypeStruct((t, GROUP_WIDTH), BF16),
        compiler_params=_cparams(("arbitrary",)),
        name="pool",
    )(u, u, u, p["pool_bd"], p["pool_scale"])


def _mix_ffn_kernel(x_ref, ota_ref, otc_ref, otd_ref, yb_ref, wout_ref, mod_ref, gpost1_ref,
                    gpre2_ref, wgu_ref, wdown_ref, gpost2_ref, o_ref):
    subs = [slice(k, k + SUB_TILE) for k in range(0, x_ref.shape[0], SUB_TILE)]
    n_att = ota_ref.shape[0] + otc_ref.shape[0] + otd_ref.shape[0]
    ys = []
    for sl in subs:
        ot = jnp.concatenate([ota_ref[:, sl], otc_ref[:, sl], otd_ref[:, sl]], axis=0)
        y = lax.dot_general(ot, wout_ref[0, 0:n_att, :], (((0,), (0,)), ((), ())), preferred_element_type=F32)
        ys.append(y + jnp.dot(yb_ref[sl, :], wout_ref[0, n_att:, :], preferred_element_type=F32))
    xs = [x_ref[sl, :] + mod_ref[0, 0, 2:3, :] * _rms(y, gpost1_ref[0]) for sl, y in zip(subs, ys)]
    gus = []
    for x in xs:
        h = _rms(x, gpre2_ref[0])
        h = h * (1.0 + mod_ref[0, 0, 4:5, :]) + mod_ref[0, 0, 3:4, :]
        gus.append(jnp.dot(h.astype(BF16), wgu_ref[0], preferred_element_type=F32))
    y2s = []
    for gu in gus:
        gate, up = gu[:, :D_FF], gu[:, D_FF:]
        act = (gate * jax.nn.sigmoid(gate) * up).astype(BF16)
        y2s.append(jnp.dot(act, wdown_ref[0], preferred_element_type=F32))
    for sl, x, y2 in zip(subs, xs, y2s):
        o_ref[sl, :] = x + mod_ref[0, 0, 5:6, :] * _rms(y2, gpost2_ref[0])


def _mix_ffn(x, ot_a, ot_c, ot_d, yb, mod, stream, layer, p):
    t = x.shape[0]
    r = min(ROW_TILE, t)
    feat = pl.BlockSpec((GROUP_WIDTH, r), lambda i: (0, i))
    lay3 = lambda i: (layer, 0, 0)
    once = pl.Buffered(1)
    return pl.pallas_call(
        _mix_ffn_kernel,
        grid=(t // r,),
        in_specs=[
            pl.BlockSpec((r, D_MODEL), lambda i: (i, 0)),
            feat, feat, feat,
            pl.BlockSpec((r, GROUP_WIDTH), lambda i: (i, 0)),
            pl.BlockSpec((1, D_MODEL, D_MODEL), lay3, pipeline_mode=once),
            pl.BlockSpec((1, 1, N_MOD, D_MODEL), lambda i: (layer, stream, 0, 0)),
            pl.BlockSpec((1, 1, D_MODEL), lay3),
            pl.BlockSpec((1, 1, D_MODEL), lay3),
            pl.BlockSpec((1, D_MODEL, 2 * D_FF), lay3, pipeline_mode=once),
            pl.BlockSpec((1, D_FF, D_MODEL), lay3, pipeline_mode=once),
            pl.BlockSpec((1, 1, D_MODEL), lay3),
        ],
        out_specs=pl.BlockSpec((r, D_MODEL), lambda i: (i, 0)),
        out_shape=jax.ShapeDtypeStruct((t, D_MODEL), F32),
        compiler_params=_cparams(("arbitrary",)),
        name="mix_ffn",
    )(x, ot_a, ot_c, ot_d, yb, p["w_out"], mod, p["g_post1"], p["g_pre2"], p["w_gu"], p["w_down"], p["g_post2"])


def _rope_tables(n):
    rows = n // GRID_W
    row = jnp.repeat(jnp.arange(rows, dtype=F32), GRID_W)
    col = jnp.tile(jnp.arange(GRID_W, dtype=F32), rows)
    out = []
    for rot_dim in (HEAD_DIM, C_ROPE):
        n_axis = rot_dim // 4
        freqs = ROPE_THETA ** (-jnp.arange(n_axis, dtype=F32) / n_axis)
        ang = jnp.concatenate([row[:, None] * freqs, col[:, None] * freqs], axis=-1)
        out += [jnp.cos(ang).T, jnp.sin(ang).T]
    return tuple(out)


def _identity_rope(n):
    return (jnp.ones((HEAD_DIM // 2, n), F32), jnp.zeros((HEAD_DIM // 2, n), F32),
            jnp.ones((C_ROPE // 2, n), F32), jnp.zeros((C_ROPE // 2, n), F32))


def _pack_params(w_in, pool_w, w_out, w_gu, w_down, c_wuq, c_wukv):
    depth = w_in.shape[0]
    pad = jnp.zeros((depth, D_MODEL, IN_PACKED - IN_WIDTH), w_in.dtype)
    w_in_p = jnp.concatenate([w_in[:, :, :KR_END], pad, w_in[:, :, KR_END:]], axis=2).astype(BF16)
    eye = jnp.eye(len(POOL_SIZES), dtype=pool_w.dtype)
    pool_bd = jnp.einsum("lgce,gh->lgche", pool_w, eye).reshape(depth, GROUP_WIDTH, GROUP_WIDTH).astype(BF16)
    w_out_p = jnp.concatenate([w_out[:, 0:256], w_out[:, 512:1024], w_out[:, 256:512]], axis=1).astype(BF16)
    return dict(w_in=w_in_p, pool_bd=pool_bd, w_out=w_out_p, w_gu=w_gu.astype(BF16), w_down=w_down.astype(BF16),
                c_wuq=c_wuq.astype(BF16), c_wukv=c_wukv.astype(BF16))


def kernel(x, c, ctx, c_ctx, w_ada, b_ada, g_pre1, g_post1, w_in, a_sink, pool_w, pool_scale, c_gq, c_wuq, c_gkv,
           c_wukv, d_gq, d_gk, w_out, g_pre2, g_post2, w_gu, w_down):
    assert x.shape[0] == 1 and ctx.shape[0] == 1
    depth = w_in.shape[0]
    n = x.shape[1]
    n_ctx = ctx.shape[1]
    p = _pack_params(w_in, pool_w, w_out, w_gu, w_down, c_wuq, c_wukv)
    row3 = lambda a: a.reshape(depth, 1, a.shape[1])
    p.update(g_pre1=row3(g_pre1), g_post1=row3(g_post1), g_pre2=row3(g_pre2), g_post2=row3(g_post2),
             pool_scale=row3(pool_scale), c_gq=row3(c_gq), c_gkv=row3(c_gkv),
             d_gq=d_gq.reshape(depth, HEAD_DIM, 1), d_gk=d_gk.reshape(depth, HEAD_DIM, 1))

    cond = jnp.zeros((8, D_MODEL), F32).at[0].set(c[0]).at[1].set(c_ctx)
    mod = _modulation(cond, w_ada, b_ada).reshape(depth, 8, N_MOD, D_MODEL)

    rope_x = _rope_tables(n)
    rope_c = _identity_rope(n_ctx)
    xs, cs = x[0], ctx[0]
    for layer in range(depth):
        last = layer == depth - 1
        sink = a_sink[layer] * LOG2E
        qta, ka, vta, ub, qtc, kc, vtc, qtd, kd, vtd = _project(xs, mod, 0, layer, p, rope_x)
        qta_c, ka_c, vta_c, ub_c, qtc_c, kc_c, vtc_c, qtd_c, kd_c, vtd_c = _project(cs, mod, 1, layer, p, rope_c)

        ot_a = _window_attention(qta, ka_c, vta_c, ka, vta, sink)
        yb = _pool(ub, layer, p)
        ot_c = _dense_attention(qtc, kc_c, vtc_c, kc, vtc, name="dense_attn_c")
        ot_d = _dense_attention(qtd, kd_c, vtd_c, kd, vtd, name="dense_attn_d")
        xs = _mix_ffn(xs, ot_a, ot_c, ot_d, yb, mod, 0, layer, p)

        if not last:
            ot_a_c = _dense_attention(qta_c, ka_c, vta_c, sink=sink, name="ctx_attn_a")
            yb_c = _pool(ub_c, layer, p)
            ot_c_c = _dense_attention(qtc_c, kc_c, vtc_c, name="ctx_attn_c")
            ot_d_c = _dense_attention(qtd_c, kd_c, vtd_c, name="ctx_attn_d")
            cs = _mix_ffn(cs, ot_a_c, ot_c_c, ot_d_c, yb_c, mod, 1, layer, p)
    return xs[None]
```

```python
import functools

import numpy as np
import jax
import jax.numpy as jnp
from jax import lax
from jax.experimental import pallas as pl
from jax.experimental.pallas import tpu as pltpu

F32 = jnp.float32
BF16 = jnp.bfloat16

D_MODEL = 1024
GRID_W = 64
GROUP_WIDTH = 256
HEAD_DIM = 64
ROPE_THETA = 10000.0
EPS = 1e-6
NEG_INF = -1e30
LOG2E = 1.4426950408889634
N_MOD = 6
N_HEADS = 4
KV_HEADS = 2
WINDOW = 128
POOL_SIZES = (2, 4, 8, 16)
POOL_GROUP = 64
C_NOPE, C_ROPE, C_V = 64, 32, 64
C_Q_RANK, C_KV_RANK = 256, 128
D_FF = 2816
IN_WIDTH = 1696
KR_END = 1184
IN_PACKED = 1792

LANE = 128
QK_PAD = 128
V_ROWS = 80
ROW_TILE = 512
SUB_TILE = 256
Q_TILE = 512
WIN_TILE = 256
WIN_STEP = 1024
FLASH_GROUP = 2
FLASH_SUB_TILES = 4
KV_BLOCK = 1024
MXU_DEPTH = 256
MAX_TRIP_BLOCKS = 6
HALO = 128
POOL_HALO = 8
VMEM_LIMIT = 56 * 1024 * 1024


def _cparams(sem):
    return pltpu.CompilerParams(dimension_semantics=sem, vmem_limit_bytes=VMEM_LIMIT)


def _rms(x, g):
    ms = jnp.mean(x * x, axis=-1, keepdims=True)
    return x * lax.rsqrt(ms + EPS) * g


def _split_bf16(x):
    hi = x.astype(BF16)
    lo = (x - hi.astype(F32)).astype(BF16)
    return hi, lo


def _mod_kernel(cond_ref, w_ref, b_ref, o_ref):
    c = cond_ref[...]
    s = c * jax.nn.sigmoid(c)
    s_hi, s_lo = _split_bf16(s)
    w_hi, w_lo = _split_bf16(w_ref[0])
    acc = jnp.dot(s_hi, w_hi, preferred_element_type=F32)
    acc += jnp.dot(s_hi, w_lo, preferred_element_type=F32)
    acc += jnp.dot(s_lo, w_hi, preferred_element_type=F32)
    o_ref[0] = acc + b_ref[0]


def _modulation(cond, w_ada, b_ada):
    depth, d, n = w_ada.shape
    tn = 1536
    return pl.pallas_call(
        _mod_kernel,
        grid=(depth, n // tn),
        in_specs=[
            pl.BlockSpec((8, d), lambda l, j: (0, 0)),
            pl.BlockSpec((1, d, tn), lambda l, j: (l, 0, j)),
            pl.BlockSpec((1, 1, tn), lambda l, j: (l, 0, j)),
        ],
        out_specs=pl.BlockSpec((1, 8, tn), lambda l, j: (l, 0, j)),
        out_shape=jax.ShapeDtypeStruct((depth, 8, n), F32),
        compiler_params=_cparams(("arbitrary", "arbitrary")),
        name="modulation",
    )(cond, w_ada, b_ada.reshape(depth, 1, n))


def _rope_t(xt, cos, sin):
    half = xt.shape[0] // 2
    x1, x2 = xt[:half], xt[half:]
    return jnp.concatenate([x1 * cos - x2 * sin, x2 * cos + x1 * sin], axis=0)


def _ones_rows(r):
    row = lax.broadcasted_iota(jnp.int32, (V_ROWS - HEAD_DIM, r), 0)
    return jnp.where(row == 0, 1.0, 0.0).astype(BF16)


def _proj_kernel(x_ref, mod_ref, gpre_ref, win_ref, cgq_ref, cwuq_ref, cgkv_ref, cwukv_ref,
                 dgq_ref, dgk_ref, cos_hd_ref, sin_hd_ref, cos_c_ref, sin_c_ref,
                 qta_ref, ka_ref, vta_ref, u_ref, qtc_ref, kc_ref, vtc_ref, qtd_ref, kd_ref, vtd_ref):
    subs = [slice(k, k + SUB_TILE) for k in range(0, x_ref.shape[0], SUB_TILE)]
    zs = []
    for sl in subs:
        h = _rms(x_ref[sl, :], gpre_ref[0])
        h = h * (1.0 + mod_ref[0, 0, 1:2, :]) + mod_ref[0, 0, 0:1, :]
        zs.append(jnp.dot(h.astype(BF16), win_ref[0], preferred_element_type=F32))
    for sl, z in zip(subs, zs):
        _proj_groups(z, sl, cgq_ref, cwuq_ref, cgkv_ref, cwukv_ref, dgq_ref, dgk_ref,
                     cos_hd_ref, sin_hd_ref, cos_c_ref, sin_c_ref,
                     qta_ref, ka_ref, vta_ref, u_ref, qtc_ref, kc_ref, vtc_ref, qtd_ref, kd_ref, vtd_ref)


def _proj_groups(z, sl, cgq_ref, cwuq_ref, cgkv_ref, cwukv_ref, dgq_ref, dgk_ref,
                 cos_hd_ref, sin_hd_ref, cos_c_ref, sin_c_ref,
                 qta_ref, ka_ref, vta_ref, u_ref, qtc_ref, kc_ref, vtc_ref, qtd_ref, kd_ref, vtd_ref):
    r = z.shape[0]
    scale_hd = HEAD_DIM ** -0.5 * LOG2E
    scale_c = (C_NOPE + C_ROPE) ** -0.5 * LOG2E
    cos_hd, sin_hd = cos_hd_ref[:, sl], sin_hd_ref[:, sl]
    cos_c, sin_c = cos_c_ref[:, sl], sin_c_ref[:, sl]
    ones = _ones_rows(r)
    zeros64 = jnp.zeros((HEAD_DIM, r), BF16)

    def put_gqa_q(ref, qt, hd):
        g = hd // (N_HEADS // KV_HEADS)
        ref[hd, g * HEAD_DIM:(g + 1) * HEAD_DIM, sl] = (qt * scale_hd).astype(BF16)
        ref[hd, (1 - g) * HEAD_DIM:(2 - g) * HEAD_DIM, sl] = zeros64

    def put_v(ref, hd, vt):
        ref[hd, 0:HEAD_DIM, sl] = vt.astype(BF16)
        ref[hd, HEAD_DIM:V_ROWS, sl] = ones

    qa_t = z[:, 0:256].T
    for hd in range(N_HEADS):
        put_gqa_q(qta_ref, _rope_t(qa_t[hd * 64:(hd + 1) * 64], cos_hd, sin_hd), hd)
    ka_t = z[:, 256:384].T
    ka_t = jnp.concatenate([_rope_t(ka_t[g * 64:(g + 1) * 64], cos_hd, sin_hd) for g in range(KV_HEADS)], axis=0)
    ka_ref[0, sl, :] = ka_t.T.astype(BF16)
    va_t = z[:, 384:512].T
    for g in range(KV_HEADS):
        put_v(vta_ref, g, va_t[g * 64:(g + 1) * 64])

    u_ref[sl, :] = z[:, 512:768]

    cq = _rms(z[:, 768:1024], cgq_ref[0])
    qc_t = jnp.dot(cq.astype(BF16), cwuq_ref[0], preferred_element_type=F32).T
    ckv = _rms(z[:, 1024:1152], cgkv_ref[0])
    kvc_t = jnp.dot(ckv.astype(BF16), cwukv_ref[0], preferred_element_type=F32).T
    kr_t = _rope_t(z[:, 1152:1280].T[0:C_ROPE], cos_c, sin_c)
    pad32 = jnp.zeros((QK_PAD - C_NOPE - C_ROPE, r), F32)
    per_head_q = C_NOPE + C_ROPE
    for hd in range(N_HEADS):
        q_nope = qc_t[hd * per_head_q:hd * per_head_q + C_NOPE]
        q_rope = _rope_t(qc_t[hd * per_head_q + C_NOPE:(hd + 1) * per_head_q], cos_c, sin_c)
        qtc_ref[hd, :, sl] = (jnp.concatenate([q_nope, q_rope, pad32], axis=0) * scale_c).astype(BF16)
        k_nope = kvc_t[hd * 128:hd * 128 + C_NOPE]
        kc_ref[hd, sl, :] = jnp.concatenate([k_nope, kr_t, pad32], axis=0).T.astype(BF16)
        put_v(vtc_ref, hd, kvc_t[hd * 128 + C_NOPE:(hd + 1) * 128])

    def head_norm(xt, g_col):
        ms = jnp.mean(xt * xt, axis=0, keepdims=True)
        return xt * lax.rsqrt(ms + EPS) * g_col

    qd_t = z[:, 1280:1536].T
    for hd in range(N_HEADS):
        q = head_norm(qd_t[hd * 64:(hd + 1) * 64], dgq_ref[0])
        put_gqa_q(qtd_ref, _rope_t(q, cos_hd, sin_hd), hd)
    kd_t = z[:, 1536:1664].T
    kd_t = jnp.concatenate(
        [_rope_t(head_norm(kd_t[g * 64:(g + 1) * 64], dgk_ref[0]), cos_hd, sin_hd) for g in range(KV_HEADS)], axis=0)
    kd_ref[0, sl, :] = kd_t.T.astype(BF16)
    vd_t = z[:, 1664:1792].T
    for g in range(KV_HEADS):
        put_v(vtd_ref, g, vd_t[g * 64:(g + 1) * 64])


def _project(x, mod, stream, layer, p, rope):
    t = x.shape[0]
    r = min(ROW_TILE, t)
    cos_hd, sin_hd, cos_c, sin_c = rope
    lay3 = lambda i: (layer, 0, 0)
    tok = lambda i: (i, 0)
    feat3 = lambda i: (0, 0, i)
    tok3 = lambda i: (0, i, 0)
    out_shapes = (
        jax.ShapeDtypeStruct((N_HEADS, QK_PAD, t), BF16),
        jax.ShapeDtypeStruct((1, t, QK_PAD), BF16),
        jax.ShapeDtypeStruct((KV_HEADS, V_ROWS, t), BF16),
        jax.ShapeDtypeStruct((t, GROUP_WIDTH), F32),
        jax.ShapeDtypeStruct((N_HEADS, QK_PAD, t), BF16),
        jax.ShapeDtypeStruct((N_HEADS, t, QK_PAD), BF16),
        jax.ShapeDtypeStruct((N_HEADS, V_ROWS, t), BF16),
        jax.ShapeDtypeStruct((N_HEADS, QK_PAD, t), BF16),
        jax.ShapeDtypeStruct((1, t, QK_PAD), BF16),
        jax.ShapeDtypeStruct((KV_HEADS, V_ROWS, t), BF16),
    )
    out_specs = (
        pl.BlockSpec((N_HEADS, QK_PAD, r), feat3),
        pl.BlockSpec((1, r, QK_PAD), tok3),
        pl.BlockSpec((KV_HEADS, V_ROWS, r), feat3),
        pl.BlockSpec((r, GROUP_WIDTH), tok),
        pl.BlockSpec((N_HEADS, QK_PAD, r), feat3),
        pl.BlockSpec((N_HEADS, r, QK_PAD), tok3),
        pl.BlockSpec((N_HEADS, V_ROWS, r), feat3),
        pl.BlockSpec((N_HEADS, QK_PAD, r), feat3),
        pl.BlockSpec((1, r, QK_PAD), tok3),
        pl.BlockSpec((KV_HEADS, V_ROWS, r), feat3),
    )
    in_specs = [
        pl.BlockSpec((r, D_MODEL), tok),
        pl.BlockSpec((1, 1, N_MOD, D_MODEL), lambda i: (layer, stream, 0, 0)),
        pl.BlockSpec((1, 1, D_MODEL), lay3),
        pl.BlockSpec((1, D_MODEL, IN_PACKED), lay3),
        pl.BlockSpec((1, 1, C_Q_RANK), lay3),
        pl.BlockSpec((1, C_Q_RANK, p["c_wuq"].shape[2]), lay3),
        pl.BlockSpec((1, 1, C_KV_RANK), lay3),
        pl.BlockSpec((1, C_KV_RANK, p["c_wukv"].shape[2]), lay3),
        pl.BlockSpec((1, HEAD_DIM, 1), lay3),
        pl.BlockSpec((1, HEAD_DIM, 1), lay3),
        pl.BlockSpec((HEAD_DIM // 2, r), lambda i: (0, i)),
        pl.BlockSpec((HEAD_DIM // 2, r), lambda i: (0, i)),
        pl.BlockSpec((C_ROPE // 2, r), lambda i: (0, i)),
        pl.BlockSpec((C_ROPE // 2, r), lambda i: (0, i)),
    ]
    return pl.pallas_call(
        _proj_kernel,
        grid=(t // r,),
        in_specs=in_specs,
        out_specs=out_specs,
        out_shape=out_shapes,
        compiler_params=_cparams(("arbitrary",)),
        name="in_proj",
    )(x, mod, p["g_pre1"], p["w_in"], p["c_gq"], p["c_wuq"], p["c_gkv"], p["c_wukv"],
      p["d_gq"], p["d_gk"], cos_hd, sin_hd, cos_c, sin_c)


def _trip_blocks(n_blocks):
    return max(b for b in range(2, MAX_TRIP_BLOCKS + 1, 2) if (n_blocks - 2) % b == 0)


def _flash_tile(q, sink, kc_ref, vtc_ref, kl_ref, vtl_ref, m_ref, acc_ref, scratch, n_blocks, emit):
    s = jnp.dot(kc_ref[0], q, preferred_element_type=F32)

    if n_blocks:
        s_refs, cm_refs, p_refs, al_refs = scratch[0:2], scratch[2:4], scratch[4:6], scratch[6:8]
        parts = KV_BLOCK // MXU_DEPTH

        def produce_part(slot, j, h):
            off = pl.multiple_of(j * KV_BLOCK + h * MXU_DEPTH, MXU_DEPTH)
            sj = jnp.dot(kl_ref[0, pl.ds(off, MXU_DEPTH), :], q, preferred_element_type=F32)
            s_refs[slot][h * MXU_DEPTH:(h + 1) * MXU_DEPTH, :] = sj

        def soften(slot):
            m_old = m_ref[...]
            m_new = jnp.maximum(m_old, cm_refs[slot][...])
            al_refs[slot][...] = jnp.exp2(m_old - m_new)
            p_refs[slot][...] = jnp.exp2(s_refs[slot][...] - m_new).astype(BF16)
            m_ref[...] = m_new
            cm_refs[1 - slot][...] = jnp.max(s_refs[1 - slot][...], axis=0, keepdims=True)

        def accumulate_part(slot, j, h):
            off = pl.multiple_of(j * KV_BLOCK + h * MXU_DEPTH, MXU_DEPTH)
            pv = jnp.dot(vtl_ref[0, :, pl.ds(off, MXU_DEPTH)], p_refs[slot][h * MXU_DEPTH:(h + 1) * MXU_DEPTH, :],
                         preferred_element_type=F32)
            scale = al_refs[slot][...] if h == 0 else 1.0
            acc_ref[...] = scale * acc_ref[...] + pv

        def stages(slot_p, j_p, slot_a, j_a):
            for h in range(parts):
                if j_p is not None:
                    produce_part(slot_p, j_p, h)
                if j_a is not None:
                    accumulate_part(slot_a, j_a, h)

        stages(0, 0, None, None)
    yield

    m = jnp.max(s, axis=0, keepdims=True)
    if sink is not None:
        m = jnp.maximum(m, sink)
    p = jnp.exp2(s - m)
    acc = jnp.dot(vtc_ref[0], p.astype(BF16), preferred_element_type=F32)
    if sink is not None:
        row = lax.broadcasted_iota(jnp.int32, acc.shape, 0)
        acc = acc + jnp.where(row == HEAD_DIM, jnp.exp2(sink - m), 0.0)

    if n_blocks:
        m_ref[...] = m
        acc_ref[...] = acc
        p_refs[1][...] = jnp.zeros(p_refs[1].shape, BF16)
        al_refs[1][...] = jnp.ones(al_refs[1].shape, F32)
        per_trip = _trip_blocks(n_blocks)
        stages(1, 1, None, None)
        cm_refs[0][...] = jnp.max(s_refs[0][...], axis=0, keepdims=True)
    yield

    def trip(t):
        for b in range(per_trip):
            j = t * per_trip + b
            soften(b % 2)
            stages(b % 2, j + 2, 1 - b % 2, jnp.maximum(j - 1, 0))

    yield (trip, (n_blocks - 2) // per_trip) if n_blocks else None

    for j in (n_blocks - 2, n_blocks - 1):
        if n_blocks:
            soften(j % 2)
            stages(None, None, 1 - j % 2, j - 1)
        yield

    if n_blocks:
        stages(None, None, (n_blocks - 1) % 2, n_blocks - 1)
        acc = acc_ref[...]
    emit(acc)
    yield


def _flash_kernel(*refs, n_blocks, has_sink, n_sub):
    refs = list(refs)
    sink = refs.pop(0)[pl.program_id(0)] if has_sink else None
    qt_ref, kc_ref, vtc_ref = refs[:3]
    kl_ref, vtl_ref = refs[3:5] if n_blocks else (None, None)
    refs = refs[5:] if n_blocks else refs[3:]
    o_ref, scratch = refs[0], refs[1:]
    per_sub = len(scratch) // n_sub
    tq = o_ref.shape[1] // n_sub

    def tile(k):
        def emit(acc):
            o_ref[:, k * tq:(k + 1) * tq] = (acc[0:HEAD_DIM] / acc[HEAD_DIM:HEAD_DIM + 1]).astype(o_ref.dtype)

        sub = scratch[k * per_sub:(k + 1) * per_sub]
        return _flash_tile(qt_ref[0, :, k * tq:(k + 1) * tq], sink, kc_ref, vtc_ref, kl_ref, vtl_ref,
                           sub[0], sub[1], sub[2:], n_blocks, emit)

    tiles = [tile(k) for k in range(n_sub)]
    groups = [tiles[k:k + FLASH_GROUP] for k in range(0, n_sub, FLASH_GROUP)]

    def advance(group):
        return [next(t) for t in group]

    def run_loop(group):
        loops = advance(group)
        if loops[0] is not None:
            def body(t, carry):
                for trip, _ in loops:
                    trip(t)
                return carry
            lax.fori_loop(0, loops[0][1], body, 0)

    advance(groups[0])
    advance(groups[0])
    run_loop(groups[0])
    for k, group in enumerate(groups):
        follower = groups[k + 1] if k + 1 < len(groups) else None
        for _ in range(2):
            if follower is not None:
                advance(follower)
            advance(group)
        advance(group)
        if follower is not None:
            run_loop(follower)


def _dense_attention(qt, k_ctx, vt_ctx, k_lat=None, vt_lat=None, sink=None, name="dense_attn"):
    n_heads, _, tq_total = qt.shape
    tq = min(Q_TILE, tq_total)
    n_sub = min(FLASH_SUB_TILES, tq_total // tq)
    step = tq * n_sub
    n_ctx = k_ctx.shape[1]
    k_rep = n_heads // k_ctx.shape[0]
    v_rep = n_heads // vt_ctx.shape[0]
    has_sink = sink is not None
    has_lat = k_lat is not None
    n_lat = k_lat.shape[1] if has_lat else 0
    n_blocks = n_lat // KV_BLOCK
    assert n_lat % (2 * KV_BLOCK) == 0 and n_blocks != 2
    args, in_specs = [], []
    if has_sink:
        args.append(sink)
        in_specs.append(pl.BlockSpec(memory_space=pltpu.SMEM))
    args += [qt, k_ctx, vt_ctx]
    in_specs += [
        pl.BlockSpec((1, QK_PAD, step), lambda h, i: (h, 0, i)),
        pl.BlockSpec((1, n_ctx, QK_PAD), lambda h, i: (h // k_rep, 0, 0)),
        pl.BlockSpec((1, V_ROWS, n_ctx), lambda h, i: (h // v_rep, 0, 0)),
    ]
    scratch = [pltpu.VMEM((1, tq), F32), pltpu.VMEM((V_ROWS, tq), F32)]
    if has_lat:
        args += [k_lat, vt_lat]
        in_specs += [
            pl.BlockSpec((1, n_lat, QK_PAD), lambda h, i: (h // k_rep, 0, 0)),
            pl.BlockSpec((1, V_ROWS, n_lat), lambda h, i: (h // v_rep, 0, 0)),
        ]
        scratch += (2 * [pltpu.VMEM((KV_BLOCK, tq), F32)] + 2 * [pltpu.VMEM((1, tq), F32)]
                    + 2 * [pltpu.VMEM((KV_BLOCK, tq), BF16)] + 2 * [pltpu.VMEM((1, tq), F32)])
    return pl.pallas_call(
        functools.partial(_flash_kernel, n_blocks=n_blocks, has_sink=has_sink, n_sub=n_sub),
        grid=(n_heads, tq_total // step),
        in_specs=in_specs,
        out_specs=pl.BlockSpec((HEAD_DIM, step), lambda h, i: (h, i)),
        out_shape=jax.ShapeDtypeStruct((n_heads * HEAD_DIM, tq_total), BF16),
        scratch_shapes=scratch * n_sub,
        compiler_params=_cparams(("arbitrary", "arbitrary")),
        name=name,
    )(*args)


def _window_kernel(sink_ref, qt_ref, kc_ref, vtc_ref, kp_ref, km_ref, kn_ref, vp_ref, vm_ref, vn_ref,
                   bias_ref, o_ref):
    i = pl.program_id(0)
    n = pl.num_programs(0)
    n_heads, n_kv = qt_ref.shape[0], vtc_ref.shape[0]
    per_kv = n_heads // n_kv
    n_tiles = WIN_STEP // WIN_TILE
    band = WIN_TILE + 2 * HALO
    dot = functools.partial(jnp.dot, preferred_element_type=F32)
    k_all = jnp.concatenate([kp_ref[0], km_ref[0], kn_ref[0]], axis=0)
    v_all = [jnp.concatenate([vp_ref[g], vm_ref[g], vn_ref[g]], axis=1) for g in range(n_kv)]
    lane_head = lax.broadcasted_iota(jnp.int32, (1, n_heads * WIN_TILE), 1) // WIN_TILE
    sink = functools.reduce(lambda acc, hd: jnp.where(lane_head == hd, sink_ref[hd], acc), range(n_heads),
                            jnp.zeros((1, n_heads * WIN_TILE), F32))
    row = lax.broadcasted_iota(jnp.int32, (band, 1), 0)

    def scores(t):
        q = jnp.concatenate([qt_ref[hd, :, t * WIN_TILE:(t + 1) * WIN_TILE] for hd in range(n_heads)], axis=1)
        s_band = dot(k_all[t * WIN_TILE:t * WIN_TILE + band], q) + bias_ref[...]
        if t == 0:
            s_band = s_band + jnp.where((row < HALO) & (i == 0), NEG_INF, 0.0)
        if t == n_tiles - 1:
            s_band = s_band + jnp.where((row >= band - HALO) & (i == n - 1), NEG_INF, 0.0)
        return dot(kc_ref[0], q), s_band

    def finish(t, ss):
        s_ctx, s_band = ss
        m = jnp.maximum(jnp.max(s_ctx, axis=0, keepdims=True), jnp.max(s_band, axis=0, keepdims=True))
        m = jnp.maximum(m, sink)
        p_ctx = jnp.exp2(s_ctx - m).astype(BF16)
        p_band = jnp.exp2(s_band - m).astype(BF16)
        sink_term = jnp.exp2(sink - m)
        for g in range(n_kv):
            cols = slice(g * per_kv * WIN_TILE, (g + 1) * per_kv * WIN_TILE)
            acc = (dot(vtc_ref[g], p_ctx[:, cols])
                   + dot(v_all[g][:, t * WIN_TILE:t * WIN_TILE + band], p_band[:, cols]))
            out = acc[0:HEAD_DIM] / (acc[HEAD_DIM:HEAD_DIM + 1] + sink_term[:, cols])
            for k in range(per_kv):
                hd = g * per_kv + k
                o_ref[hd * HEAD_DIM:(hd + 1) * HEAD_DIM, t * WIN_TILE:(t + 1) * WIN_TILE] = (
                    out[:, k * WIN_TILE:(k + 1) * WIN_TILE].astype(o_ref.dtype))

    ss = scores(0)
    for t in range(n_tiles):
        nxt = scores(t + 1) if t + 1 < n_tiles else None
        finish(t, ss)
        ss = nxt


def _window_bias(n_heads):
    r = np.arange(WIN_TILE + 2 * HALO)[:, None] - HALO
    c = np.arange(WIN_TILE)[None, :]
    bias = np.where(np.abs(c - r) <= WINDOW, 0.0, NEG_INF).astype(np.float32)
    return np.tile(bias, (1, n_heads))


def _window_attention(qt, k_ctx, vt_ctx, k_lat, vt_lat, sink):
    n_heads, _, t = qt.shape
    n_ctx = k_ctx.shape[1]
    n_kv = vt_ctx.shape[0]
    per = WIN_STEP // HALO
    last = t // HALO - 1
    prev_blk = lambda i: jnp.maximum(i * per - 1, 0)
    next_blk = lambda i: jnp.minimum(i * per + per, last)
    in_specs = [
        pl.BlockSpec(memory_space=pltpu.SMEM),
        pl.BlockSpec((n_heads, QK_PAD, WIN_STEP), lambda i: (0, 0, i)),
        pl.BlockSpec((1, n_ctx, QK_PAD), lambda i: (0, 0, 0)),
        pl.BlockSpec((n_kv, V_ROWS, n_ctx), lambda i: (0, 0, 0)),
        pl.BlockSpec((1, HALO, QK_PAD), lambda i: (0, prev_blk(i), 0)),
        pl.BlockSpec((1, WIN_STEP, QK_PAD), lambda i: (0, i, 0)),
        pl.BlockSpec((1, HALO, QK_PAD), lambda i: (0, next_blk(i), 0)),
        pl.BlockSpec((n_kv, V_ROWS, HALO), lambda i: (0, 0, prev_blk(i))),
        pl.BlockSpec((n_kv, V_ROWS, WIN_STEP), lambda i: (0, 0, i)),
        pl.BlockSpec((n_kv, V_ROWS, HALO), lambda i: (0, 0, next_blk(i))),
        pl.BlockSpec((WIN_TILE + 2 * HALO, n_heads * WIN_TILE), lambda i: (0, 0)),
    ]
    return pl.pallas_call(
        _window_kernel,
        grid=(t // WIN_STEP,),
        in_specs=in_specs,
        out_specs=pl.BlockSpec((n_heads * HEAD_DIM, WIN_STEP), lambda i: (0, i)),
        out_shape=jax.ShapeDtypeStruct((n_heads * HEAD_DIM, t), BF16),
        compiler_params=_cparams(("arbitrary",)),
        name="window_attn",
    )(sink, qt, k_ctx, vt_ctx, k_lat, k_lat, k_lat, vt_lat, vt_lat, vt_lat, jnp.asarray(_window_bias(n_heads)))


def _pool_kernel(u_ref, up_ref, un_ref, w_ref, scale_ref, o_ref, *, t_total):
    i = pl.program_id(0)
    n = pl.num_programs(0)
    r = u_ref.shape[0]
    u = u_ref[...]
    prev = jnp.where(i > 0, up_ref[...], 0.0)
    nxt = jnp.where(i < n - 1, un_ref[...], 0.0)
    ext = jnp.concatenate([prev, u, nxt], axis=0)
    rows = ext.shape[0]

    def shifted(x, k):
        return pltpu.roll(x, k % rows, axis=0)

    def window_sums(e):
        s2 = shifted(e, 1) + e
        s4 = shifted(s2, 1) + shifted(s2, -1)
        s8 = shifted(s4, 2) + shifted(s4, -2)
        s16 = shifted(s8, 4) + shifted(s8, -4)
        return s2, s4, s8, s16

    low_group = lax.broadcasted_iota(jnp.int32, (rows, LANE), 1) < POOL_GROUP
    s2, s4, _, _ = window_sums(ext[:, 0:LANE])
    _, _, s8, s16 = window_sums(ext[:, LANE:2 * LANE])
    total = jnp.concatenate([jnp.where(low_group, s2, s4), jnp.where(low_group, s8, s16)], axis=1)
    total = total[POOL_HALO:POOL_HALO + r]

    lane = lax.broadcasted_iota(jnp.int32, (r, GROUP_WIDTH), 1)
    pos = lax.broadcasted_iota(jnp.int32, (r, GROUP_WIDTH), 0) + i * r
    half = jnp.zeros((r, GROUP_WIDTH), jnp.int32)
    for gi, size in enumerate(POOL_SIZES):
        half = jnp.where((lane >= gi * POOL_GROUP) & (lane < (gi + 1) * POOL_GROUP), size // 2, half)
    count = jnp.minimum(pos + half, t_total) - jnp.maximum(pos - half, 0)
    y = total / count.astype(F32) - u
    o_ref[...] = (jnp.dot(y.astype(BF16), w_ref[0], preferred_element_type=F32) * scale_ref[0]).astype(o_ref.dtype)


def _pool(u, layer, p):
    t = u.shape[0]
    r = min(ROW_TILE, t)
    per = r // POOL_HALO
    last = t // POOL_HALO - 1
    return pl.pallas_call(
        functools.partial(_pool_kernel, t_total=t),
        grid=(t // r,),
        in_specs=[
            pl.BlockSpec((r, GROUP_WIDTH), lambda i: (i, 0)),
            pl.BlockSpec((POOL_HALO, GROUP_WIDTH), lambda i: (jnp.maximum(i * per - 1, 0), 0)),
            pl.BlockSpec((POOL_HALO, GROUP_WIDTH), lambda i: (jnp.minimum(i * per + per, last), 0)),
            pl.BlockSpec((1, GROUP_WIDTH, GROUP_WIDTH), lambda i: (layer, 0, 0)),
            pl.BlockSpec((1, 1, GROUP_WIDTH), lambda i: (layer, 0, 0)),
        ],
        out_specs=pl.BlockSpec((r, GROUP_WIDTH), lambda i: (i, 0)),
        out_shape=jax.ShapeDtypeStruct((t, GROUP_WIDTH), BF16),
        compiler_params=_cparams(("arbitrary",)),
        name="pool",
    )(u, u, u, p["pool_bd"], p["pool_scale"])


def _mix_ffn_kernel(x_ref, ota_ref, otc_ref, otd_ref, yb_ref, wout_ref, mod_ref, gpost1_ref,
                    gpre2_ref, wgu_ref, wdown_ref, gpost2_ref, o_ref):
    subs = [slice(k, k + SUB_TILE) for k in range(0, x_ref.shape[0], SUB_TILE)]
    n_att = ota_ref.shape[0] + otc_ref.shape[0] + otd_ref.shape[0]
    ys = []
    for sl in subs:
        ot = jnp.concatenate([ota_ref[:, sl], otc_ref[:, sl], otd_ref[:, sl]], axis=0)
        y = lax.dot_general(ot, wout_ref[0, 0:n_att, :], (((0,), (0,)), ((), ())), preferred_element_type=F32)
        ys.append(y + jnp.dot(yb_ref[sl, :], wout_ref[0, n_att:, :], preferred_element_type=F32))
    xs = [x_ref[sl, :] + mod_ref[0, 0, 2:3, :] * _rms(y, gpost1_ref[0]) for sl, y in zip(subs, ys)]
    gus = []
    for x in xs:
        h = _rms(x, gpre2_ref[0])
        h = h * (1.0 + mod_ref[0, 0, 4:5, :]) + mod_ref[0, 0, 3:4, :]
        gus.append(jnp.dot(h.astype(BF16), wgu_ref[0], preferred_element_type=F32))
    y2s = []
    for gu in gus:
        gate, up = gu[:, :D_FF], gu[:, D_FF:]
        act = (gate * jax.nn.sigmoid(gate) * up).astype(BF16)
        y2s.append(jnp.dot(act, wdown_ref[0], preferred_element_type=F32))
    for sl, x, y2 in zip(subs, xs, y2s):
        o_ref[sl, :] = x + mod_ref[0, 0, 5:6, :] * _rms(y2, gpost2_ref[0])


def _mix_ffn(x, ot_a, ot_c, ot_d, yb, mod, stream, layer, p):
    t = x.shape[0]
    r = min(ROW_TILE, t)
    feat = pl.BlockSpec((GROUP_WIDTH, r), lambda i: (0, i))
    lay3 = lambda i: (layer, 0, 0)
    once = pl.Buffered(1)
    return pl.pallas_call(
        _mix_ffn_kernel,
        grid=(t // r,),
        in_specs=[
            pl.BlockSpec((r, D_MODEL), lambda i: (i, 0)),
            feat, feat, feat,
            pl.BlockSpec((r, GROUP_WIDTH), lambda i: (i, 0)),
            pl.BlockSpec((1, D_MODEL, D_MODEL), lay3, pipeline_mode=once),
            pl.BlockSpec((1, 1, N_MOD, D_MODEL), lambda i: (layer, stream, 0, 0)),
            pl.BlockSpec((1, 1, D_MODEL), lay3),
            pl.BlockSpec((1, 1, D_MODEL), lay3),
            pl.BlockSpec((1, D_MODEL, 2 * D_FF), lay3, pipeline_mode=once),
            pl.BlockSpec((1, D_FF, D_MODEL), lay3, pipeline_mode=once),
            pl.BlockSpec((1, 1, D_MODEL), lay3),
        ],
        out_specs=pl.BlockSpec((r, D_MODEL), lambda i: (i, 0)),
        out_shape=jax.ShapeDtypeStruct((t, D_MODEL), F32),
        compiler_params=_cparams(("arbitrary",)),
        name="mix_ffn",
    )(x, ot_a, ot_c, ot_d, yb, p["w_out"], mod, p["g_post1"], p["g_pre2"], p["w_gu"], p["w_down"], p["g_post2"])


def _rope_tables(n):
    rows = n // GRID_W
    row = jnp.repeat(jnp.arange(rows, dtype=F32), GRID_W)
    col = jnp.tile(jnp.arange(GRID_W, dtype=F32), rows)
    out = []
    for rot_dim in (HEAD_DIM, C_ROPE):
        n_axis = rot_dim // 4
        freqs = ROPE_THETA ** (-jnp.arange(n_axis, dtype=F32) / n_axis)
        ang = jnp.concatenate([row[:, None] * freqs, col[:, None] * freqs], axis=-1)
        out += [jnp.cos(ang).T, jnp.sin(ang).T]
    return tuple(out)


def _identity_rope(n):
    return (jnp.ones((HEAD_DIM // 2, n), F32), jnp.zeros((HEAD_DIM // 2, n), F32),
            jnp.ones((C_ROPE // 2, n), F32), jnp.zeros((C_ROPE // 2, n), F32))


def _pack_params(w_in, pool_w, w_out, w_gu, w_down, c_wuq, c_wukv):
    depth = w_in.shape[0]
    pad = jnp.zeros((depth, D_MODEL, IN_PACKED - IN_WIDTH), w_in.dtype)
    w_in_p = jnp.concatenate([w_in[:, :, :KR_END], pad, w_in[:, :, KR_END:]], axis=2).astype(BF16)
    eye = jnp.eye(len(POOL_SIZES), dtype=pool_w.dtype)
    pool_bd = jnp.einsum("lgce,gh->lgche", pool_w, eye).reshape(depth, GROUP_WIDTH, GROUP_WIDTH).astype(BF16)
    w_out_p = jnp.concatenate([w_out[:, 0:256], w_out[:, 512:1024], w_out[:, 256:512]], axis=1).astype(BF16)
    return dict(w_in=w_in_p, pool_bd=pool_bd, w_out=w_out_p, w_gu=w_gu.astype(BF16), w_down=w_down.astype(BF16),
                c_wuq=c_wuq.astype(BF16), c_wukv=c_wukv.astype(BF16))


def kernel(x, c, ctx, c_ctx, w_ada, b_ada, g_pre1, g_post1, w_in, a_sink, pool_w, pool_scale, c_gq, c_wuq, c_gkv,
           c_wukv, d_gq, d_gk, w_out, g_pre2, g_post2, w_gu, w_down):
    assert x.shape[0] == 1 and ctx.shape[0] == 1
    depth = w_in.shape[0]
    n = x.shape[1]
    n_ctx = ctx.shape[1]
    p = _pack_params(w_in, pool_w, w_out, w_gu, w_down, c_wuq, c_wukv)
    row3 = lambda a: a.reshape(depth, 1, a.shape[1])
    p.update(g_pre1=row3(g_pre1), g_post1=row3(g_post1), g_pre2=row3(g_pre2), g_post2=row3(g_post2),
             pool_scale=row3(pool_scale), c_gq=row3(c_gq), c_gkv=row3(c_gkv),
             d_gq=d_gq.reshape(depth, HEAD_DIM, 1), d_gk=d_gk.reshape(depth, HEAD_DIM, 1))

    cond = jnp.zeros((8, D_MODEL), F32).at[0].set(c[0]).at[1].set(c_ctx)
    mod = _modulation(cond, w_ada, b_ada).reshape(depth, 8, N_MOD, D_MODEL)

    rope_x = _rope_tables(n)
    rope_c = _identity_rope(n_ctx)
    xs, cs = x[0], ctx[0]
    for layer in range(depth):
        last = layer == depth - 1
        sink = a_sink[layer] * LOG2E
        qta, ka, vta, ub, qtc, kc, vtc, qtd, kd, vtd = _project(xs, mod, 0, layer, p, rope_x)
        qta_c, ka_c, vta_c, ub_c, qtc_c, kc_c, vtc_c, qtd_c, kd_c, vtd_c = _project(cs, mod, 1, layer, p, rope_c)

        ot_a = _window_attention(qta, ka_c, vta_c, ka, vta, sink)
        yb = _pool(ub, layer, p)
        ot_c = _dense_attention(qtc, kc_c, vtc_c, kc, vtc, name="dense_attn_c")
        ot_d = _dense_attention(qtd, kd_c, vtd_c, kd, vtd, name="dense_attn_d")
        xs = _mix_ffn(xs, ot_a, ot_c, ot_d, yb, mod, 0, layer, p)

        if not last:
            ot_a_c = _dense_attention(qta_c, ka_c, vta_c, sink=sink, name="ctx_attn_a")
            yb_c = _pool(ub_c, layer, p)
            ot_c_c = _dense_attention(qtc_c, kc_c, vtc_c, name="ctx_attn_c")
            ot_d_c = _dense_attention(qtd_c, kd_c, vtd_c, name="ctx_attn_d")
            cs = _mix_ffn(cs, ot_a_c, ot_c_c, ot_d_c, yb_c, mod, 1, layer, p)
    return xs[None]
```
